```python
import math
import jax, jax.numpy as jnp
from jax import lax
import numpy as np

D_MODEL = 4096
BATCH = 2
SEQ = 8192
DEPTH = 2

GRID_W = 64
CTX_LEN = 256
HEAD_DIM = 128
ROPE_THETA = 10000.0
Q_BLOCK = 128
EPS = 1e-6
N_MOD = 6

A_Q_HEADS = 16
A_KV_HEADS = 4
A_GROUP = A_Q_HEADS // A_KV_HEADS
A_Q_DIM = A_Q_HEADS * HEAD_DIM
A_KV_DIM = A_KV_HEADS * HEAD_DIM
B_WIDTH = 2048
CONV_W = 3
C_HEADS = 8
C_QK_DIM = C_HEADS * 2 * HEAD_DIM
C_V_DIM = C_HEADS * 2 * HEAD_DIM

N_BRANCH = 3
BRANCH_DIM = 2048
KV_COLS = 2 * A_KV_DIM + C_QK_DIM + C_V_DIM
REST_COLS = A_Q_DIM + C_QK_DIM + 3 * B_WIDTH + N_BRANCH * D_MODEL
IN_COLS = KV_COLS + REST_COLS
D_FF = 2 * D_MODEL

kernel_name = "hybrid_parallel_gqa_shortconv_diffattn_dit"


def rms_norm(x, g):
    xf = x.astype(jnp.float32)
    y = xf * lax.rsqrt(jnp.mean(xf * xf, axis=-1, keepdims=True) + EPS)
    return (y * g.astype(jnp.float32)).astype(x.dtype)


def modulate(h, shift, scale):
    return h * (1 + scale) + shift


def axial_rope_tables(rows):
    row = jnp.broadcast_to(jnp.arange(rows, dtype=jnp.float32)[:, None], (rows, GRID_W)).reshape(-1)
    col = jnp.broadcast_to(jnp.arange(GRID_W, dtype=jnp.float32)[None, :], (rows, GRID_W)).reshape(-1)
    quarter = HEAD_DIM // 4
    freqs = ROPE_THETA ** (-jnp.arange(quarter, dtype=jnp.float32) / quarter)
    ang_r = row[:, None] * freqs
    ang_c = col[:, None] * freqs
    ang = jnp.concatenate([ang_r, ang_r, ang_c, ang_c], axis=-1)
    return jnp.cos(ang), jnp.sin(ang)


def apply_rope(x, cos, sin):
    shape = (1, x.shape[1]) + (1,) * (x.ndim - 3) + (HEAD_DIM,)
    xf = x.astype(jnp.float32)
    xs = xf.reshape(xf.shape[:-1] + (2, 2, HEAD_DIM // 4))
    rot = jnp.stack([-xs[..., 1, :], xs[..., 0, :]], axis=-2).reshape(xf.shape)
    return (xf * cos.reshape(shape) + rot * sin.reshape(shape)).astype(x.dtype)


def dwconv3(x, w):
    return lax.conv_general_dilated(
        x, w[:, None, :].astype(x.dtype), window_strides=(1,),
        padding=((CONV_W // 2, CONV_W // 2),),
        dimension_numbers=('NWC', 'WIO', 'NWC'), feature_group_count=x.shape[-1])


def gqa_attend(q, k, v):
    s = jnp.einsum('bqhgd,bkhd->bhgqk', q, k) * (HEAD_DIM ** -0.5)
    p = jax.nn.softmax(s.astype(jnp.float32), axis=-1).astype(v.dtype)
    return jnp.einsum('bhgqk,bkhd->bqhgd', p, v)


def diff_attend(q, k, v, lam):
    s = jnp.einsum('bqhjd,bkhjd->bhjqk', q, k) * (HEAD_DIM ** -0.5)
    p = jax.nn.softmax(s.astype(jnp.float32), axis=-1)
    p = (p[:, :, 0] - lam * p[:, :, 1]).astype(v.dtype)
    return jnp.einsum('bhqk,bkhe->bqhe', p, v)


def blocked_queries(attend, q, *rest):
    b, s = q.shape[:2]
    nblk = s // Q_BLOCK
    qb = jnp.moveaxis(q.reshape((b, nblk, Q_BLOCK) + q.shape[2:]), 1, 0)
    ob = lax.map(lambda blk: attend(blk, *rest), qb)
    return jnp.moveaxis(ob, 0, 1).reshape((b, s) + ob.shape[3:])


def split_kv(p_kv, g_ka, g_kc, rope):
    b, t = p_kv.shape[:2]
    ak, av, ck, cv = jnp.split(p_kv, [A_KV_DIM, 2 * A_KV_DIM, 2 * A_KV_DIM + C_QK_DIM], axis=-1)
    ak = rms_norm(ak.reshape(b, t, A_KV_HEADS, HEAD_DIM), g_ka)
    ck = rms_norm(ck.reshape(b, t, C_HEADS, 2, HEAD_DIM), g_kc)
    if rope is not None:
        ak = apply_rope(ak, *rope)
        ck = apply_rope(ck, *rope)
    return (ak, av.reshape(b, t, A_KV_HEADS, HEAD_DIM), ck, cv.reshape(b, t, C_HEADS, 2 * HEAD_DIM))


def mixer_output(p_rest, kv, rope, g_qa, g_qc, w_conv_b, lam, lambda_init, g_subln, w_branch, w_out):
    b, t = p_rest.shape[:2]
    aq, cq, bb, bc, bx, gates = jnp.split(
        p_rest, [A_Q_DIM, A_Q_DIM + C_QK_DIM, A_Q_DIM + C_QK_DIM + B_WIDTH,
                 A_Q_DIM + C_QK_DIM + 2 * B_WIDTH, A_Q_DIM + C_QK_DIM + 3 * B_WIDTH], axis=-1)
    ak, av, ck, cv = kv
    aq = rms_norm(aq.reshape(b, t, A_KV_HEADS, A_GROUP, HEAD_DIM), g_qa)
    cq = rms_norm(cq.reshape(b, t, C_HEADS, 2, HEAD_DIM), g_qc)
    if rope is not None:
        aq = apply_rope(aq, *rope)
        cq = apply_rope(cq, *rope)
        ya = blocked_queries(gqa_attend, aq, ak, av)
        yc = blocked_queries(lambda qb, kk, vv: diff_attend(qb, kk, vv, lam), cq, ck, cv)
    else:
        ya = gqa_attend(aq, ak, av)
        yc = diff_attend(cq, ck, cv, lam)
    ya = ya.reshape(b, t, A_Q_DIM)
    yc = (rms_norm(yc, g_subln) * (1 - lambda_init)).reshape(b, t, C_V_DIM)
    yb = bb * dwconv3(bc * bx, w_conv_b)
    branches = jnp.stack([ya, yb, yc], axis=2)
    proj = jnp.einsum('btnc,ncd->btnd', branches, w_branch)
    g = jax.nn.sigmoid(gates.reshape(b, t, N_BRANCH, D_MODEL))
    return jnp.sum(g * proj, axis=2) @ w_out


def conv_glu(h, w_up, w_conv_ffn, w_down):
    u, g = jnp.split(h @ w_up, 2, axis=-1)
    return (jax.nn.silu(dwconv3(g, w_conv_ffn)) * u) @ w_down


def setup_inputs(seed: int = 0) -> dict:
    key = jax.random.key(seed)
    ks = jax.random.split(key, 24)
    L, D = DEPTH, D_MODEL

    def nrm(k, shape, scale):
        return jax.random.normal(k, shape, jnp.float32) * scale

    return {
        "x": nrm(ks[0], (BATCH, SEQ, D), 1.0),
        "c": nrm(ks[1], (BATCH, D), 1.0),
        "ctx": nrm(ks[2], (BATCH, CTX_LEN, D), 1.0),
        "c_ctx": nrm(ks[3], (D,), 1.0),
        "w_mod": nrm(ks[4], (L, D, N_MOD * D), 0.5 * D ** -0.5),
        "b_mod": nrm(ks[5], (L, N_MOD * D), 0.02),
        "g_norm_mix": 1.0 + nrm(ks[6], (L, D), 0.02),
        "g_norm_ffn": 1.0 + nrm(ks[7], (L, D), 0.02),
        "w_in": nrm(ks[8], (L, D, IN_COLS), D ** -0.5),
        "g_qa": 1.0 + nrm(ks[9], (L, HEAD_DIM), 0.02),
        "g_ka": 1.0 + nrm(ks[10], (L, HEAD_DIM), 0.02),
        "g_qc": 1.0 + nrm(ks[11], (L, 2, HEAD_DIM), 0.02),
        "g_kc": 1.0 + nrm(ks[12], (L, 2, HEAD_DIM), 0.02),
        "w_conv_b": nrm(ks[13], (L, CONV_W, B_WIDTH), CONV_W ** -0.5),
        "lam_qk": nrm(ks[14], (L, 4, HEAD_DIM), 0.1),
        "g_subln": 1.0 + nrm(ks[15], (L, 2 * HEAD_DIM), 0.02),
        "w_branch": nrm(ks[16], (L, N_BRANCH, BRANCH_DIM, D), BRANCH_DIM ** -0.5),
        "w_out": nrm(ks[17], (L, D, D), D ** -0.5),
        "w_up": nrm(ks[18], (L, D, 2 * D_FF), D ** -0.5),
        "w_conv_ffn": nrm(ks[19], (L, CONV_W, D_FF), CONV_W ** -0.5),
        "w_down": nrm(ks[20], (L, D_FF, D), D_FF ** -0.5),
    }


def reference(x, c, ctx, c_ctx, w_mod, b_mod, g_norm_mix, g_norm_ffn, w_in, g_qa, g_ka, g_qc, g_kc,
              w_conv_b, lam_qk, g_subln, w_branch, w_out, w_up, w_conv_ffn, w_down):
    n_tok = x.shape[1]
    rows = n_tok // GRID_W
    rope = axial_rope_tables(rows)
    s_c = jax.nn.silu(c)
    s_ctx = jax.nn.silu(c_ctx)
    for l in range(DEPTH):
        last = l == DEPTH - 1
        lambda_init = 0.8 - 0.6 * math.exp(-0.3 * l)
        lq = lam_qk[l].astype(jnp.float32)
        lam = jnp.exp(jnp.sum(lq[0] * lq[1])) - jnp.exp(jnp.sum(lq[2] * lq[3])) + lambda_init

        mx = (s_c @ w_mod[l] + b_mod[l])[:, None, :]
        sh_m, sc_m, gt_m, sh_f, sc_f, gt_f = jnp.split(mx, N_MOD, axis=-1)
        mc = s_ctx @ w_mod[l] + b_mod[l]
        csh_m, csc_m, cgt_m, csh_f, csc_f, cgt_f = jnp.split(mc, N_MOD, axis=-1)

        h_x = modulate(rms_norm(x, g_norm_mix[l]), sh_m, sc_m)
        h_c = modulate(rms_norm(ctx, g_norm_mix[l]), csh_m, csc_m)
        p_x = h_x @ w_in[l]
        if last:
            p_c_kv = h_c @ w_in[l][:, :KV_COLS]
        else:
            p_c = h_c @ w_in[l]
            p_c_kv = p_c[..., :KV_COLS]
        kv_c = split_kv(p_c_kv, g_ka[l], g_kc[l], None)
        kv_x = split_kv(p_x[..., :KV_COLS], g_ka[l], g_kc[l], rope)
        kv_all = tuple(jnp.concatenate([kc_, kx_], axis=1) for kc_, kx_ in zip(kv_c, kv_x))
        mix_x = mixer_output(p_x[..., KV_COLS:], kv_all, rope, g_qa[l], g_qc[l], w_conv_b[l],
                             lam, lambda_init, g_subln[l], w_branch[l], w_out[l])
        x = x + gt_m * mix_x
        if not last:
            mix_c = mixer_output(p_c[..., KV_COLS:], kv_c, None, g_qa[l], g_qc[l], w_conv_b[l],
                                 lam, lambda_init, g_subln[l], w_branch[l], w_out[l])
            ctx = ctx + cgt_m * mix_c

        f_x = modulate(rms_norm(x, g_norm_ffn[l]), sh_f, sc_f)
        x = x + gt_f * conv_glu(f_x, w_up[l], w_conv_ffn[l], w_down[l])
        if not last:
            f_c = modulate(rms_norm(ctx, g_norm_ffn[l]), csh_f, csc_f)
            ctx = ctx + cgt_f * conv_glu(f_c, w_up[l], w_conv_ffn[l], w_down[l])
    return x
```

```python
import functools
import math

import jax
import jax.numpy as jnp
from jax import lax
from jax.experimental import pallas as pl
from jax.experimental.pallas import tpu as pltpu

GRID_W = 64
HEAD_DIM = 128
ROPE_THETA = 10000.0
EPS = 1e-6
N_MOD = 6
A_Q_HEADS = 16
A_KV_HEADS = 4
B_WIDTH = 2048
C_HEADS = 8
N_BRANCH = 3

LANES = 128
BF16_SUBLANES = 16
VMEM_LIMIT = 56 << 20

F32 = jnp.float32
BF16 = jnp.bfloat16


def _tile(n, target, mult):
    best = None
    t = mult
    while t <= min(n, target):
        if n % t == 0:
            best = t
        t += mult
    return n if best is None else best


def _params(*semantics):
    return pltpu.CompilerParams(dimension_semantics=semantics, vmem_limit_bytes=VMEM_LIMIT)


class _Stream:
    def __init__(self, n_seq, seq_len, rows_per_mod):
        self.n_seq, self.seq_len, self.rows_per_mod = n_seq, seq_len, rows_per_mod
        self.rows = n_seq * seq_len


def _modvec_kernel(c_ref, w_ref, b_ref, o_ref):
    c = c_ref[...]
    s = c * jax.nn.sigmoid(c)
    o_ref[...] = jnp.dot(s.astype(BF16), w_ref[...].astype(BF16),
                         preferred_element_type=F32) + b_ref[...]


def _modvec(cvec, w_mod, b_mod):
    n_layer, d, n = w_mod.shape
    rows = cvec.shape[0]
    tn = _tile(n, 512, LANES)
    return pl.pallas_call(
        _modvec_kernel,
        out_shape=jax.ShapeDtypeStruct((n_layer, rows, n), F32),
        grid=(n_layer, n // tn),
        in_specs=[pl.BlockSpec((rows, d), lambda l, j: (0, 0)),
                  pl.BlockSpec((None, d, tn), lambda l, j: (l, 0, j)),
                  pl.BlockSpec((None, 1, tn), lambda l, j: (l, 0, j))],
        out_specs=pl.BlockSpec((None, rows, tn), lambda l, j: (l, 0, j)),
        compiler_params=_params("arbitrary", "arbitrary"),
        name="modvec",
    )(cvec, w_mod, b_mod.reshape(n_layer, 1, n))


def _norm_mod_kernel(x_ref, g_ref, sh_ref, sc_ref, o_ref):
    x = x_ref[...]
    ms = jnp.mean(x * x, axis=-1, keepdims=True)
    y = x * lax.rsqrt(ms + EPS) * g_ref[...]
    o_ref[...] = (y * (1.0 + sc_ref[...]) + sh_ref[...]).astype(o_ref.dtype)


def _norm_mod(st, x, g, mod, shift_idx, scale_idx):
    m, d = x.shape
    tm = _tile(math.gcd(st.seq_len, st.rows_per_mod), 256, BF16_SUBLANES)
    tpm = st.rows_per_mod // tm
    return pl.pallas_call(
        _norm_mod_kernel,
        out_shape=jax.ShapeDtypeStruct((m, d), BF16),
        grid=(m // tm,),
        in_specs=[pl.BlockSpec((tm, d), lambda i: (i, 0)),
                  pl.BlockSpec((1, d), lambda i: (0, 0)),
                  pl.BlockSpec((None, 1, d), lambda i: (i // tpm, 0, shift_idx)),
                  pl.BlockSpec((None, 1, d), lambda i: (i // tpm, 0, scale_idx))],
        out_specs=pl.BlockSpec((tm, d), lambda i: (i, 0)),
        compiler_params=_params("arbitrary"),
        name="norm_mod",
    )(x, g.reshape(1, d), mod, mod)


def _mm_kernel(a_ref, w_ref, o_ref):
    o_ref[...] = jnp.dot(a_ref[...], w_ref[...], preferred_element_type=F32).astype(o_ref.dtype)


def _mm(a, w, n_cols=None, tm_target=1024, tn_target=512):
    m, k = a.shape
    n = w.shape[1] if n_cols is None else n_cols
    tm = _tile(m, tm_target, BF16_SUBLANES)
    tn = _tile(n, tn_target, LANES)
    return pl.pallas_call(
        _mm_kernel,
        out_shape=jax.ShapeDtypeStruct((m, n), BF16),
        grid=(m // tm, n // tn),
        in_specs=[pl.BlockSpec((tm, k), lambda i, j: (i, 0)),
                  pl.BlockSpec((k, tn), lambda i, j: (0, j))],
        out_specs=pl.BlockSpec((tm, tn), lambda i, j: (i, j)),
        compiler_params=_params("arbitrary", "arbitrary"),
        name="mm",
    )(a, w)


def _norm_rope_head(x, g, cos, sin_signed, first_half, out_scale):
    ms = jnp.mean(x * x, axis=-1, keepdims=True)
    y = x * lax.rsqrt(ms + EPS) * g
    if cos is not None:
        rot = jnp.where(first_half, pltpu.roll(y, HEAD_DIM - HEAD_DIM // 4, 1),
                        pltpu.roll(y, HEAD_DIM // 4, 1))
        y = y * cos + rot * sin_signed
    if out_scale != 1.0:
        y = y * out_scale
    return y


def _prep_kernel(*refs, segs, rope):
    n_out = len(segs)
    p_ref, g_ref = refs[0], refs[1]
    if rope:
        cos = refs[2][...]
        sin_signed = refs[3][...]
        outs = refs[4:4 + n_out]
        lane = lax.broadcasted_iota(jnp.int32, (1, HEAD_DIM), 1)
        first_half = (lane % (HEAD_DIM // 2)) < (HEAD_DIM // 4)
    else:
        cos = sin_signed = first_half = None
        outs = refs[2:2 + n_out]
    for (kind, in_col, width, out_index, gain_col, out_scale) in segs:
        o_ref = outs[out_index]
        if kind == "copy":
            o_ref[...] = p_ref[:, in_col:in_col + width]
            continue
        for h in range(width // HEAD_DIM):
            x = p_ref[:, in_col + h * HEAD_DIM:in_col + (h + 1) * HEAD_DIM].astype(F32)
            g = g_ref[:, gain_col + h * HEAD_DIM:gain_col + (h + 1) * HEAD_DIM]
            y = _norm_rope_head(x, g, cos, sin_signed, first_half, out_scale)
            o_ref[:, h * HEAD_DIM:(h + 1) * HEAD_DIM] = y.astype(o_ref.dtype)


def _prep(st, p, gains, segs, in_width, rope_tables, kv_outs, kv_row_block, q_widths):
    m = p.shape[0]
    tm = _tile(st.seq_len, 256, BF16_SUBLANES)
    tps = st.seq_len // tm
    rope = rope_tables is not None
    in_specs = [pl.BlockSpec((tm, in_width), lambda i: (i, 0)),
                pl.BlockSpec((1, gains.shape[1]), lambda i: (0, 0))]
    args = [p, gains]
    if rope:
        in_specs += [pl.BlockSpec((tm, HEAD_DIM), lambda i: (i % tps, 0))] * 2
        args += list(rope_tables)
    out_shapes, out_specs, aliases = [], [], {}
    for arr, (shape, width) in kv_outs:
        out_shapes.append(jax.ShapeDtypeStruct(shape, BF16))
        out_specs.append(pl.BlockSpec((None, tm, width),
                                      lambda i: (i // tps, kv_row_block + i % tps, 0)))
        if arr is not None:
            aliases[len(args)] = len(out_shapes) - 1
            in_specs.append(pl.BlockSpec(memory_space=pl.ANY))
            args.append(arr)
    for width in q_widths:
        out_shapes.append(jax.ShapeDtypeStruct((m, width), BF16))
        out_specs.append(pl.BlockSpec((tm, width), lambda i: (i, 0)))
    return pl.pallas_call(
        functools.partial(_prep_kernel_aliased, segs=segs, rope=rope, n_alias=len(aliases)),
        out_shape=out_shapes,
        grid=(m // tm,),
        in_specs=in_specs,
        out_specs=out_specs,
        input_output_aliases=aliases,
        compiler_params=_params("arbitrary"),
        name="prep",
    )(*args)


def _prep_kernel_aliased(*refs, segs, rope, n_alias):
    n_in = 2 + (2 if rope else 0)
    _prep_kernel(*refs[:n_in], *refs[n_in + n_alias:], segs=segs, rope=rope)


def _flash(q, k_ref, v_ref, k_col, v_col, v_width, tk, n_chunks):
    tq = q.shape[0]

    def body(c, carry):
        m_prev, l_prev, acc_prev = carry
        start = pl.multiple_of(c * tk, tk)
        k = k_ref[pl.ds(start, tk), k_col:k_col + HEAD_DIM]
        v = v_ref[pl.ds(start, tk), v_col:v_col + v_width]
        s = lax.dot_general(q, k, (((1,), (1,)), ((), ())), preferred_element_type=F32)
        m_new = jnp.maximum(m_prev, jnp.max(s, axis=-1, keepdims=True))
        alpha = jnp.exp2(m_prev - m_new)
        p = jnp.exp2(s - m_new)
        l_new = alpha * l_prev + jnp.sum(p, axis=-1, keepdims=True)
        acc_new = alpha * acc_prev + jnp.dot(p.astype(BF16), v, preferred_element_type=F32)
        return m_new, l_new, acc_new

    init = (jnp.full((tq, 1), -jnp.inf, F32), jnp.zeros((tq, 1), F32),
            jnp.zeros((tq, v_width), F32))
    _, l, acc = lax.fori_loop(0, n_chunks, body, init)
    return acc, l


def _attn_a_kernel(q_ref, k_ref, v_ref, o_ref, *, group, tk, n_chunks):
    for g in range(group):
        q = q_ref[:, g * HEAD_DIM:(g + 1) * HEAD_DIM]
        acc, l = _flash(q, k_ref, v_ref, 0, 0, HEAD_DIM, tk, n_chunks)
        o_ref[:, g * HEAD_DIM:(g + 1) * HEAD_DIM] = (acc / l).astype(o_ref.dtype)


def _attn_c_kernel(q_ref, k_ref, v_ref, lam_ref, g_ref, o_ref, *, tk, n_chunks, lambda_init):
    lq = lam_ref[...]
    lam = (jnp.exp(jnp.sum(lq[0:1] * lq[1:2], axis=-1, keepdims=True))
           - jnp.exp(jnp.sum(lq[2:3] * lq[3:4], axis=-1, keepdims=True)) + lambda_init)
    acc0, l0 = _flash(q_ref[:, 0:HEAD_DIM], k_ref, v_ref, 0, 0, 2 * HEAD_DIM, tk, n_chunks)
    acc1, l1 = _flash(q_ref[:, HEAD_DIM:2 * HEAD_DIM], k_ref, v_ref, HEAD_DIM, 0, 2 * HEAD_DIM,
                      tk, n_chunks)
    y = acc0 / l0 - lam * (acc1 / l1)
    ms = jnp.mean(y * y, axis=-1, keepdims=True)
    y = y * lax.rsqrt(ms + EPS) * g_ref[...]
    o_ref[...] = (y * (1.0 - lambda_init)).astype(o_ref.dtype)


def _kv_chunk(kv_len):
    return _tile(kv_len, 1024, LANES)


def _attn_a(q, k, v, n_batch, q_len, kv_len, kv_block):
    group = A_Q_HEADS // A_KV_HEADS
    gw = group * HEAD_DIM
    tq = _tile(q_len, 256, BF16_SUBLANES)
    nq = q_len // tq
    tk = _kv_chunk(kv_len)
    kernel = functools.partial(_attn_a_kernel, group=group, tk=tk, n_chunks=kv_len // tk)
    return pl.pallas_call(
        kernel,
        out_shape=jax.ShapeDtypeStruct(q.shape, BF16),
        grid=(n_batch, A_KV_HEADS, nq),
        in_specs=[pl.BlockSpec((tq, gw), lambda b, h, i: (b * nq + i, h)),
                  pl.BlockSpec((None, kv_len, HEAD_DIM), lambda b, h, i: (b, kv_block, h)),
                  pl.BlockSpec((None, kv_len, HEAD_DIM), lambda b, h, i: (b, kv_block, h))],
        out_specs=pl.BlockSpec((tq, gw), lambda b, h, i: (b * nq + i, h)),
        compiler_params=_params("arbitrary", "arbitrary", "arbitrary"),
        name="attn_gqa",
    )(q, k, v)


def _attn_c(q, k, v, lam_qk, g_subln, lambda_init, n_batch, q_len, kv_len, kv_block):
    hw = 2 * HEAD_DIM
    tq = _tile(q_len, 256, BF16_SUBLANES)
    nq = q_len // tq
    tk = _kv_chunk(kv_len)
    kernel = functools.partial(_attn_c_kernel, tk=tk, n_chunks=kv_len // tk,
                               lambda_init=lambda_init)
    return pl.pallas_call(
        kernel,
        out_shape=jax.ShapeDtypeStruct(q.shape, BF16),
        grid=(n_batch, C_HEADS, nq),
        in_specs=[pl.BlockSpec((tq, hw), lambda b, h, i: (b * nq + i, h)),
                  pl.BlockSpec((None, kv_len, hw), lambda b, h, i: (b, kv_block, h)),
                  pl.BlockSpec((None, kv_len, hw), lambda b, h, i: (b, kv_block, h)),
                  pl.BlockSpec(lam_qk.shape, lambda b, h, i: (0, 0)),
                  pl.BlockSpec((1, hw), lambda b, h, i: (0, 0))],
        out_specs=pl.BlockSpec((tq, hw), lambda b, h, i: (b * nq + i, h)),
        compiler_params=_params("arbitrary", "arbitrary", "arbitrary"),
        name="attn_diff",
    )(q, k, v, lam_qk, g_subln.reshape(1, hw))


def _conv3(c, c_prev_row, c_next_row, w_ref):
    tm = c.shape[0]
    row = lax.broadcasted_iota(jnp.int32, (tm, 1), 0)
    prev = jnp.where(row == 0, c_prev_row, pltpu.roll(c, 1, 0))
    nxt = jnp.where(row == tm - 1, c_next_row, pltpu.roll(c, tm - 1, 0))
    return w_ref[0:1, :] * prev + w_ref[1:2, :] * c + w_ref[2:3, :] * nxt


def _halo_rows(prev_val, next_val, tiles_per_seq):
    ti = pl.program_id(0) % tiles_per_seq
    prev_row = jnp.where(ti == 0, 0.0, prev_val)
    next_row = jnp.where(ti == tiles_per_seq - 1, 0.0, next_val)
    return prev_row, next_row


def _shortconv_kernel(bb_ref, bc_ref, bx_ref, bcp_ref, bxp_ref, bcn_ref, bxn_ref, w_ref, o_ref,
                      *, tiles_per_seq):
    last = BF16_SUBLANES - 1
    c = bc_ref[...].astype(F32) * bx_ref[...].astype(F32)
    cp = bcp_ref[last:last + 1, :].astype(F32) * bxp_ref[last:last + 1, :].astype(F32)
    cn = bcn_ref[0:1, :].astype(F32) * bxn_ref[0:1, :].astype(F32)
    cp, cn = _halo_rows(cp, cn, tiles_per_seq)
    y = bb_ref[...].astype(F32) * _conv3(c, cp, cn, w_ref)
    o_ref[...] = y.astype(o_ref.dtype)


def _glu_kernel(u_ref, g_ref, gp_ref, gn_ref, w_ref, o_ref, *, tiles_per_seq):
    last = BF16_SUBLANES - 1
    g = g_ref[...].astype(F32)
    gp, gn = _halo_rows(gp_ref[last:last + 1, :].astype(F32), gn_ref[0:1, :].astype(F32),
                        tiles_per_seq)
    z = _conv3(g, gp, gn, w_ref)
    y = z * jax.nn.sigmoid(z) * u_ref[...].astype(F32)
    o_ref[...] = y.astype(o_ref.dtype)


def _conv_specs(m, tm, wb, col_block):
    hb = tm // BF16_SUBLANES
    n_hb = m // BF16_SUBLANES
    main = pl.BlockSpec((tm, wb), lambda i, j: (i, col_block + j))
    prev = pl.BlockSpec((BF16_SUBLANES, wb),
                        lambda i, j: (jnp.maximum(i * hb - 1, 0), col_block + j))
    nxt = pl.BlockSpec((BF16_SUBLANES, wb),
                       lambda i, j: (jnp.minimum((i + 1) * hb, n_hb - 1), col_block + j))
    return main, prev, nxt


def _shortconv(st, p, w_conv, bb_col, bc_col, bx_col):
    m = p.shape[0]
    width = w_conv.shape[1]
    tm = _tile(st.seq_len, 256, BF16_SUBLANES)
    wb = _tile(math.gcd(math.gcd(bb_col, bc_col), math.gcd(bx_col, width)), 1024, LANES)
    bb = pl.BlockSpec((tm, wb), lambda i, j: (i, bb_col // wb + j))
    bc, bcp, bcn = _conv_specs(m, tm, wb, bc_col // wb)
    bx, bxp, bxn = _conv_specs(m, tm, wb, bx_col // wb)
    return pl.pallas_call(
        functools.partial(_shortconv_kernel, tiles_per_seq=st.seq_len // tm),
        out_shape=jax.ShapeDtypeStruct((m, width), BF16),
        grid=(m // tm, width // wb),
        in_specs=[bb, bc, bx, bcp, bxp, bcn, bxn,
                  pl.BlockSpec((w_conv.shape[0], wb), lambda i, j: (0, j))],
        out_specs=pl.BlockSpec((tm, wb), lambda i, j: (i, j)),
        compiler_params=_params("arbitrary", "arbitrary"),
        name="shortconv",
    )(p, p, p, p, p, p, p, w_conv)


def _glu(st, ug, w_conv):
    m = ug.shape[0]
    d_ff = w_conv.shape[1]
    tm = _tile(st.seq_len, 256, BF16_SUBLANES)
    wb = _tile(d_ff, 1024, LANES)
    u = pl.BlockSpec((tm, wb), lambda i, j: (i, j))
    g, gp, gn = _conv_specs(m, tm, wb, d_ff // wb)
    return pl.pallas_call(
        functools.partial(_glu_kernel, tiles_per_seq=st.seq_len // tm),
        out_shape=jax.ShapeDtypeStruct((m, d_ff), BF16),
        grid=(m // tm, d_ff // wb),
        in_specs=[u, g, gp, gn, pl.BlockSpec((w_conv.shape[0], wb), lambda i, j: (0, j))],
        out_specs=pl.BlockSpec((tm, wb), lambda i, j: (i, j)),
        compiler_params=_params("arbitrary", "arbitrary"),
        name="glu",
    )(ug, ug, ug, ug, w_conv)


def _branch_kernel(ya_ref, yb_ref, yc_ref, w0_ref, w1_ref, w2_ref, g0_ref, g1_ref, g2_ref, o_ref):
    acc = jax.nn.sigmoid(g0_ref[...].astype(F32)) * jnp.dot(
        ya_ref[...], w0_ref[...], preferred_element_type=F32)
    acc += jax.nn.sigmoid(g1_ref[...].astype(F32)) * jnp.dot(
        yb_ref[...], w1_ref[...], preferred_element_type=F32)
    acc += jax.nn.sigmoid(g2_ref[...].astype(F32)) * jnp.dot(
        yc_ref[...], w2_ref[...], preferred_element_type=F32)
    o_ref[...] = acc.astype(o_ref.dtype)


def _branch_merge(ya, yb, yc, w_branch, p, gates_col):
    m, k = ya.shape
    d = w_branch.shape[2]
    tm = _tile(m, 512, BF16_SUBLANES)
    tn = _tile(math.gcd(gates_col, d), 512, LANES)
    a_spec = pl.BlockSpec((tm, k), lambda i, j: (i, 0))

    def w_spec(n):
        return pl.BlockSpec((None, k, tn), lambda i, j: (n, 0, j))

    def g_spec(n):
        return pl.BlockSpec((tm, tn), lambda i, j: (i, (gates_col + n * d) // tn + j))

    return pl.pallas_call(
        _branch_kernel,
        out_shape=jax.ShapeDtypeStruct((m, d), BF16),
        grid=(m // tm, d // tn),
        in_specs=[a_spec, a_spec, a_spec, w_spec(0), w_spec(1), w_spec(2),
                  g_spec(0), g_spec(1), g_spec(2)],
        out_specs=pl.BlockSpec((tm, tn), lambda i, j: (i, j)),
        compiler_params=_params("arbitrary", "arbitrary"),
        name="branch_merge",
    )(ya, yb, yc, w_branch, w_branch, w_branch, p, p, p)


def _mm_resid_kernel(a_ref, w_ref, x_ref, gate_ref, o_ref):
    y = jnp.dot(a_ref[...], w_ref[...], preferred_element_type=F32)
    o_ref[...] = x_ref[...] + gate_ref[...] * y


def _mm_resid(st, a, w, x, mod, gate_idx):
    m, k = a.shape
    d = w.shape[1]
    tm = _tile(st.rows_per_mod, 512, BF16_SUBLANES)
    tn = _tile(d, 512, LANES)
    tpm = st.rows_per_mod // tm
    gate_block = gate_idx * (d // tn)
    return pl.pallas_call(
        _mm_resid_kernel,
        out_shape=jax.ShapeDtypeStruct((m, d), F32),
        grid=(m // tm, d // tn),
        in_specs=[pl.BlockSpec((tm, k), lambda i, j: (i, 0)),
                  pl.BlockSpec((k, tn), lambda i, j: (0, j)),
                  pl.BlockSpec((tm, tn), lambda i, j: (i, j)),
                  pl.BlockSpec((None, 1, tn), lambda i, j: (i // tpm, 0, gate_block + j))],
        out_specs=pl.BlockSpec((tm, tn), lambda i, j: (i, j)),
        compiler_params=_params("arbitrary", "arbitrary"),
        name="mm_resid",
    )(a, w, x, mod)


def _rope_tables(n_tok):
    rows = n_tok // GRID_W
    row = jnp.broadcast_to(jnp.arange(rows, dtype=F32)[:, None], (rows, GRID_W)).reshape(-1)
    col = jnp.broadcast_to(jnp.arange(GRID_W, dtype=F32)[None, :], (rows, GRID_W)).reshape(-1)
    quarter = HEAD_DIM // 4
    freqs = ROPE_THETA ** (-jnp.arange(quarter, dtype=F32) / quarter)
    ang_r = row[:, None] * freqs
    ang_c = col[:, None] * freqs
    ang = jnp.concatenate([ang_r, ang_r, ang_c, ang_c], axis=-1)
    lane = jnp.arange(HEAD_DIM)
    sign = jnp.where((lane % (HEAD_DIM // 2)) < quarter, -1.0, 1.0).astype(F32)
    return jnp.cos(ang), jnp.sin(ang) * sign


def kernel(x, c, ctx, c_ctx, w_mod, b_mod, g_norm_mix, g_norm_ffn, w_in, g_qa, g_ka, g_qc, g_kc,
           w_conv_b, lam_qk, g_subln, w_branch, w_out, w_up, w_conv_ffn, w_down):
    n_batch, n_tok, d = x.shape
    ctx_len = ctx.shape[1]
    depth = w_in.shape[0]
    d_ff = w_down.shape[1]
    a_q_dim = A_Q_HEADS * HEAD_DIM
    a_kv_dim = A_KV_HEADS * HEAD_DIM
    c_dim = C_HEADS * 2 * HEAD_DIM
    kv_cols = 2 * a_kv_dim + 2 * c_dim
    qk_cols = kv_cols + a_q_dim + c_dim
    bb_col, bc_col, bx_col = qk_cols, qk_cols + B_WIDTH, qk_cols + 2 * B_WIDTH
    gates_col = qk_cols + 3 * B_WIDTH
    kv_tok = n_tok + ctx_len
    assert n_tok % ctx_len == 0, "context keys must form one key block of the combined key array"

    lat = _Stream(n_batch, n_tok, n_tok)
    con = _Stream(n_batch, ctx_len, n_batch * ctx_len)

    mod_rows = 8
    cvec = jnp.zeros((mod_rows, d), F32).at[:n_batch].set(c).at[n_batch].set(c_ctx)
    mod_all = _modvec(cvec, w_mod, b_mod)
    rope = _rope_tables(n_tok)

    q_scale = HEAD_DIM ** -0.5 * math.log2(math.e)
    ak_col, av_col, ck_col, cv_col = 0, a_kv_dim, 2 * a_kv_dim, 2 * a_kv_dim + c_dim
    aq_col, cq_col = kv_cols, kv_cols + a_q_dim
    gk_a, gk_c, gq_a, gq_c = 0, a_kv_dim, a_kv_dim + c_dim, a_kv_dim + c_dim + a_q_dim
    kv_segs = (("norm", ak_col, a_kv_dim, 0, gk_a, 1.0), ("copy", av_col, a_kv_dim, 1, 0, 1.0),
               ("norm", ck_col, c_dim, 2, gk_c, 1.0), ("copy", cv_col, c_dim, 3, 0, 1.0))
    q_segs = (("norm", aq_col, a_q_dim, 4, gq_a, q_scale), ("norm", cq_col, c_dim, 5, gq_c, q_scale))
    kv_widths = (a_kv_dim, a_kv_dim, c_dim, c_dim)

    xs = x.reshape(n_batch * n_tok, d)
    cs = ctx.reshape(n_batch * ctx_len, d)
    for l in range(depth):
        last = l == depth - 1
        lambda_init = 0.8 - 0.6 * math.exp(-0.3 * l)
        mod = mod_all[l]
        mod_x = mod[:n_batch].reshape(n_batch, 1, N_MOD * d)
        mod_c = mod[n_batch:n_batch + 1].reshape(1, 1, N_MOD * d)
        w_in_l = w_in[l].astype(BF16)
        w_branch_l = w_branch[l].astype(BF16)
        w_out_l = w_out[l].astype(BF16)
        w_up_l = w_up[l].astype(BF16)
        w_down_l = w_down[l].astype(BF16)
        gains = jnp.concatenate([
            jnp.tile(g_ka[l], A_KV_HEADS), jnp.tile(g_kc[l].reshape(-1), C_HEADS),
            jnp.tile(g_qa[l], A_Q_HEADS), jnp.tile(g_qc[l].reshape(-1), C_HEADS)]).reshape(1, -1)

        h_x = _norm_mod(lat, xs, g_norm_mix[l], mod_x, 0, 1)
        h_c = _norm_mod(con, cs, g_norm_mix[l], mod_c, 0, 1)
        p_x = _mm(h_x, w_in_l)
        p_c = _mm(h_c, w_in_l, n_cols=kv_cols if last else None)

        kv_shapes = [(None, ((n_batch, kv_tok, w), w)) for w in kv_widths]
        outs = _prep(lat, p_x, gains, kv_segs + q_segs, qk_cols, rope, kv_shapes, 0,
                     (a_q_dim, c_dim))
        ka, va, kc, vc, qa_x, qc_x = outs
        kv_arrs = [(arr, ((n_batch, kv_tok, w), w)) for arr, w in zip((ka, va, kc, vc), kv_widths)]
        if last:
            ka, va, kc, vc = _prep(con, p_c, gains, kv_segs, kv_cols, None, kv_arrs,
                                   n_tok // _tile(ctx_len, 256, BF16_SUBLANES), ())
        else:
            ka, va, kc, vc, qa_c, qc_c = _prep(
                con, p_c, gains, kv_segs + q_segs, qk_cols, None, kv_arrs,
                n_tok // _tile(ctx_len, 256, BF16_SUBLANES), (a_q_dim, c_dim))

        def mixer(st, p, qa, qc, x_in, mod_s, q_len, kv_len, kv_block):
            ya = _attn_a(qa, ka, va, n_batch, q_len, kv_len, kv_block)
            yc = _attn_c(qc, kc, vc, lam_qk[l], g_subln[l], lambda_init, n_batch, q_len, kv_len,
                         kv_block)
            yb = _shortconv(st, p, w_conv_b[l], bb_col, bc_col, bx_col)
            merged = _branch_merge(ya, yb, yc, w_branch_l, p, gates_col)
            return _mm_resid(st, merged, w_out_l, x_in, mod_s, 2)

        def channel_mixer(st, x_in, mod_s):
            f = _norm_mod(st, x_in, g_norm_ffn[l], mod_s, 3, 4)
            ug = _mm(f, w_up_l)
            act = _glu(st, ug, w_conv_ffn[l])
            return _mm_resid(st, act, w_down_l, x_in, mod_s, 5)

        xs = mixer(lat, p_x, qa_x, qc_x, xs, mod_x, n_tok, kv_tok, 0)
        if not last:
            cs = mixer(con, p_c, qa_c, qc_c, cs, mod_c, ctx_len, ctx_len, n_tok // ctx_len)
        xs = channel_mixer(lat, xs, mod_x)
        if not last:
            cs = channel_mixer(con, cs, mod_c)
    return xs.reshape(n_batch, n_tok, d)
```

```python
import functools
import math

import jax
import jax.numpy as jnp
from jax import lax
from jax.experimental import pallas as pl
from jax.experimental.pallas import tpu as pltpu

GRID_W = 64
HEAD_DIM = 128
ROPE_THETA = 10000.0
EPS = 1e-6
N_MOD = 6
A_Q_HEADS = 16
A_KV_HEADS = 4
B_WIDTH = 2048
C_HEADS = 8
N_BRANCH = 3

LANES = 128
BF16_SUBLANES = 16
VMEM_LIMIT = 56 << 20

F32 = jnp.float32
BF16 = jnp.bfloat16


def _tile(n, target, mult):
    best = None
    t = mult
    while t <= min(n, target):
        if n % t == 0:
            best = t
        t += mult
    return n if best is None else best


def _params(*semantics):
    return pltpu.CompilerParams(dimension_semantics=semantics, vmem_limit_bytes=VMEM_LIMIT)


class _Stream:
    def __init__(self, n_seq, seq_len, rows_per_mod):
        self.n_seq, self.seq_len, self.rows_per_mod = n_seq, seq_len, rows_per_mod
        self.rows = n_seq * seq_len


def _modvec_kernel(c_ref, w_ref, b_ref, o_ref):
    c = c_ref[...]
    s = c * jax.nn.sigmoid(c)
    o_ref[...] = jnp.dot(s.astype(BF16), w_ref[...].astype(BF16),
                         preferred_element_type=F32) + b_ref[...]


def _modvec(cvec, w_mod, b_mod):
    n_layer, d, n = w_mod.shape
    rows = cvec.shape[0]
    tn = _tile(n, 512, LANES)
    return pl.pallas_call(
        _modvec_kernel,
        out_shape=jax.ShapeDtypeStruct((n_layer, rows, n), F32),
        grid=(n_layer, n // tn),
        in_specs=[pl.BlockSpec((rows, d), lambda l, j: (0, 0)),
                  pl.BlockSpec((None, d, tn), lambda l, j: (l, 0, j)),
                  pl.BlockSpec((None, 1, tn), lambda l, j: (l, 0, j))],
        out_specs=pl.BlockSpec((None, rows, tn), lambda l, j: (l, 0, j)),
        compiler_params=_params("arbitrary", "arbitrary"),
        name="modvec",
    )(cvec, w_mod, b_mod.reshape(n_layer, 1, n))


def _norm_mod_kernel(x_ref, g_ref, sh_ref, sc_ref, o_ref):
    x = x_ref[...]
    ms = jnp.mean(x * x, axis=-1, keepdims=True)
    y = x * lax.rsqrt(ms + EPS) * g_ref[...]
    o_ref[...] = (y * (1.0 + sc_ref[...]) + sh_ref[...]).astype(o_ref.dtype)


def _norm_mod(st, x, g, mod, shift_idx, scale_idx):
    m, d = x.shape
    tm = _tile(math.gcd(st.seq_len, st.rows_per_mod), 256, BF16_SUBLANES)
    tpm = st.rows_per_mod // tm
    return pl.pallas_call(
        _norm_mod_kernel,
        out_shape=jax.ShapeDtypeStruct((m, d), BF16),
        grid=(m // tm,),
        in_specs=[pl.BlockSpec((tm, d), lambda i: (i, 0)),
                  pl.BlockSpec((1, d), lambda i: (0, 0)),
                  pl.BlockSpec((None, 1, d), lambda i: (i // tpm, 0, shift_idx)),
                  pl.BlockSpec((None, 1, d), lambda i: (i // tpm, 0, scale_idx))],
        out_specs=pl.BlockSpec((tm, d), lambda i: (i, 0)),
        compiler_params=_params("arbitrary"),
        name="norm_mod",
    )(x, g.reshape(1, d), mod, mod)


def _mm_kernel(a_ref, w_ref, o_ref):
    o_ref[...] = jnp.dot(a_ref[...], w_ref[...], preferred_element_type=F32).astype(o_ref.dtype)


def _mm(a, w, n_cols=None, tm_target=1024, tn_target=512):
    m, k = a.shape
    n = w.shape[1] if n_cols is None else n_cols
    tm = _tile(m, tm_target, BF16_SUBLANES)
    tn = _tile(n, tn_target, LANES)
    return pl.pallas_call(
        _mm_kernel,
        out_shape=jax.ShapeDtypeStruct((m, n), BF16),
        grid=(m // tm, n // tn),
        in_specs=[pl.BlockSpec((tm, k), lambda i, j: (i, 0)),
                  pl.BlockSpec((k, tn), lambda i, j: (0, j))],
        out_specs=pl.BlockSpec((tm, tn), lambda i, j: (i, j)),
        compiler_params=_params("arbitrary", "arbitrary"),
        name="mm",
    )(a, w)


def _norm_rope_head(x, g, cos, sin_signed, first_half, out_scale):
    ms = jnp.mean(x * x, axis=-1, keepdims=True)
    y = x * lax.rsqrt(ms + EPS) * g
    if cos is not None:
        rot = jnp.where(first_half, pltpu.roll(y, HEAD_DIM - HEAD_DIM // 4, 1),
                        pltpu.roll(y, HEAD_DIM // 4, 1))
        y = y * cos + rot * sin_signed
    if out_scale != 1.0:
        y = y * out_scale
    return y


def _prep_kernel(*refs, segs, rope):
    n_out = len(segs)
    p_ref, g_ref = refs[0], refs[1]
    if rope:
        cos = refs[2][...]
        sin_signed = refs[3][...]
        outs = refs[4:4 + n_out]
        lane = lax.broadcasted_iota(jnp.int32, (1, HEAD_DIM), 1)
        first_half = (lane % (HEAD_DIM // 2)) < (HEAD_DIM // 4)
    else:
        cos = sin_signed = first_half = None
        outs = refs[2:2 + n_out]
    for (kind, in_col, width, out_index, gain_col, out_scale) in segs:
        o_ref = outs[out_index]
        if kind == "copy":
            o_ref[...] = p_ref[:, in_col:in_col + width]
            continue
        for h in range(width // HEAD_DIM):
            x = p_ref[:, in_col + h * HEAD_DIM:in_col + (h + 1) * HEAD_DIM].astype(F32)
            g = g_ref[:, gain_col + h * HEAD_DIM:gain_col + (h + 1) * HEAD_DIM]
            y = _norm_rope_head(x, g, cos, sin_signed, first_half, out_scale)
            o_ref[:, h * HEAD_DIM:(h + 1) * HEAD_DIM] = y.astype(o_ref.dtype)


def _prep(st, p, gains, segs, in_width, rope_tables, kv_outs, kv_row_block, q_widths):
    m = p.shape[0]
    tm = _tile(st.seq_len, 256, BF16_SUBLANES)
    tps = st.seq_len // tm
    rope = rope_tables is not None
    in_specs = [pl.BlockSpec((tm, in_width), lambda i: (i, 0)),
                pl.BlockSpec((1, gains.shape[1]), lambda i: (0, 0))]
    args = [p, gains]
    if rope:
        in_specs += [pl.BlockSpec((tm, HEAD_DIM), lambda i: (i % tps, 0))] * 2
        args += list(rope_tables)
    out_shapes, out_specs, aliases = [], [], {}
    for arr, (shape, width) in kv_outs:
        out_shapes.append(jax.ShapeDtypeStruct(shape, BF16))
        out_specs.append(pl.BlockSpec((None, tm, width),
                                      lambda i: (i // tps, kv_row_block + i % tps, 0)))
        if arr is not None:
            aliases[len(args)] = len(out_shapes) - 1
            in_specs.append(pl.BlockSpec(memory_space=pl.ANY))
            args.append(arr)
    for width in q_widths:
        out_shapes.append(jax.ShapeDtypeStruct((m, width), BF16))
        out_specs.append(pl.BlockSpec((tm, width), lambda i: (i, 0)))
    return pl.pallas_call(
        functools.partial(_prep_kernel_aliased, segs=segs, rope=rope, n_alias=len(aliases)),
        out_shape=out_shapes,
        grid=(m // tm,),
        in_specs=in_specs,
        out_specs=out_specs,
        input_output_aliases=aliases,
        compiler_params=_params("arbitrary"),
        name="prep",
    )(*args)


def _prep_kernel_aliased(*refs, segs, rope, n_alias):
    n_in = 2 + (2 if rope else 0)
    _prep_kernel(*refs[:n_in], *refs[n_in + n_alias:], segs=segs, rope=rope)


def _flash(q_ref, q_parts, k_ref, v_ref, tk, n_chunks, s_ref, m_ref, l_ref, acc_ref):
    m_ref[...] = jnp.full(m_ref.shape, -jnp.inf, F32)
    l_ref[...] = jnp.zeros(l_ref.shape, F32)
    acc_ref[...] = jnp.zeros(acc_ref.shape, F32)

    def scores(c, slot):
        start = pl.multiple_of(c * tk, tk)
        for row0, n_rows, k_col in q_parts:
            k = k_ref[pl.ds(start, tk), k_col:k_col + HEAD_DIM]
            s_ref[slot, row0:row0 + n_rows, :] = lax.dot_general(
                q_ref[row0:row0 + n_rows, :], k, (((1,), (1,)), ((), ())),
                preferred_element_type=F32)

    n_col = tk // LANES
    v_rep = acc_ref.shape[1] // LANES

    def update(c, slot):
        start = pl.multiple_of(c * tk, tk)
        s_blocks = [s_ref[slot, :, j * LANES:(j + 1) * LANES] for j in range(n_col)]
        m_cur = s_blocks[0]
        for sb in s_blocks[1:]:
            m_cur = jnp.maximum(m_cur, sb)
        m_prev = m_ref[...]
        m_new = jnp.maximum(m_prev, jnp.max(m_cur, axis=-1, keepdims=True))
        alpha = jnp.exp2(m_prev - m_new)
        p_blocks = [jnp.exp2(sb - m_new) for sb in s_blocks]
        l_new = alpha * l_ref[...]
        for pb in p_blocks:
            l_new = l_new + pb
        l_ref[...] = l_new
        p = jnp.concatenate([pb.astype(BF16) for pb in p_blocks], axis=1)
        v = v_ref[pl.ds(start, tk), :]
        alpha_acc = alpha if v_rep == 1 else jnp.concatenate([alpha] * v_rep, axis=1)
        acc_ref[...] = alpha_acc * acc_ref[...] + jnp.dot(p, v, preferred_element_type=F32)
        m_ref[...] = m_new

    scores(0, 0)
    n_pairs = (n_chunks - 1) // 2

    def pair(i, carry):
        c = 2 * i
        scores(c + 1, 1)
        update(c, 0)
        scores(c + 2, 0)
        update(c + 1, 1)
        return carry

    if n_pairs > 0:
        lax.fori_loop(0, n_pairs, pair, 0)
    c = 2 * n_pairs
    if n_chunks - 1 - c == 1:
        scores(c + 1, 1)
        update(c, 0)
        update(c + 1, 1)
    else:
        update(c, 0)


def _attn_a_kernel(q_ref, k_ref, v_ref, o_ref, qs_ref, s_ref, m_ref, l_ref, acc_ref,
                   *, group, tk, n_chunks):
    tq = q_ref.shape[0]
    for g in range(group):
        qs_ref[g * tq:(g + 1) * tq, :] = q_ref[:, g * HEAD_DIM:(g + 1) * HEAD_DIM]
    _flash(qs_ref, ((0, group * tq, 0),), k_ref, v_ref, tk, n_chunks, s_ref, m_ref, l_ref, acc_ref)
    for g in range(group):
        rows = slice(g * tq, (g + 1) * tq)
        l = jnp.sum(l_ref[rows, :], axis=-1, keepdims=True)
        o_ref[:, g * HEAD_DIM:(g + 1) * HEAD_DIM] = (acc_ref[rows, :] / l).astype(o_ref.dtype)


def _attn_c_kernel(q_ref, k_ref, v_ref, lam_ref, g_ref, o_ref, qs_ref, s_ref, m_ref, l_ref,
                   acc_ref, *, tk, n_chunks, lambda_init):
    tq = q_ref.shape[0]
    for j in range(2):
        qs_ref[j * tq:(j + 1) * tq, :] = q_ref[:, j * HEAD_DIM:(j + 1) * HEAD_DIM]
    _flash(qs_ref, ((0, tq, 0), (tq, tq, HEAD_DIM)), k_ref, v_ref, tk, n_chunks, s_ref, m_ref,
           l_ref, acc_ref)
    lq = lam_ref[...]
    lam = (jnp.exp(jnp.sum(lq[0:1] * lq[1:2], axis=-1, keepdims=True))
           - jnp.exp(jnp.sum(lq[2:3] * lq[3:4], axis=-1, keepdims=True)) + lambda_init)
    l0 = jnp.sum(l_ref[0:tq, :], axis=-1, keepdims=True)
    l1 = jnp.sum(l_ref[tq:2 * tq, :], axis=-1, keepdims=True)
    y = (acc_ref[0:tq, :] / l0) - lam * (acc_ref[tq:2 * tq, :] / l1)
    ms = jnp.mean(y * y, axis=-1, keepdims=True)
    y = y * lax.rsqrt(ms + EPS) * g_ref[...]
    o_ref[...] = (y * (1.0 - lambda_init)).astype(o_ref.dtype)


def _kv_chunk(kv_len):
    return _tile(kv_len, 1024, LANES)


def _flash_scratch(rows, tk, v_width):
    return [pltpu.VMEM((rows, HEAD_DIM), BF16), pltpu.VMEM((2, rows, tk), F32),
            pltpu.VMEM((rows, LANES), F32), pltpu.VMEM((rows, LANES), F32),
            pltpu.VMEM((rows, v_width), F32)]


def _attn_a(q, k, v, n_batch, q_len, kv_len, kv_block):
    group = A_Q_HEADS // A_KV_HEADS
    gw = group * HEAD_DIM
    tq = _tile(q_len, 1024 // group, BF16_SUBLANES)
    nq = q_len // tq
    tk = _kv_chunk(kv_len)
    kernel = functools.partial(_attn_a_kernel, group=group, tk=tk, n_chunks=kv_len // tk)
    return pl.pallas_call(
        kernel,
        out_shape=jax.ShapeDtypeStruct(q.shape, BF16),
        grid=(n_batch, A_KV_HEADS, nq),
        in_specs=[pl.BlockSpec((tq, gw), lambda b, h, i: (b * nq + i, h)),
                  pl.BlockSpec((None, kv_len, HEAD_DIM), lambda b, h, i: (b, kv_block, h)),
                  pl.BlockSpec((None, kv_len, HEAD_DIM), lambda b, h, i: (b, kv_block, h))],
        out_specs=pl.BlockSpec((tq, gw), lambda b, h, i: (b * nq + i, h)),
        scratch_shapes=_flash_scratch(group * tq, tk, HEAD_DIM),
        compiler_params=_params("arbitrary", "arbitrary", "arbitrary"),
        name="attn_gqa",
    )(q, k, v)


def _attn_c(q, k, v, lam_qk, g_subln, lambda_init, n_batch, q_len, kv_len, kv_block):
    hw = 2 * HEAD_DIM
    tq = _tile(q_len, 512, BF16_SUBLANES)
    nq = q_len // tq
    tk = _kv_chunk(kv_len)
    kernel = functools.partial(_attn_c_kernel, tk=tk, n_chunks=kv_len // tk,
                               lambda_init=lambda_init)
    return pl.pallas_call(
        kernel,
        out_shape=jax.ShapeDtypeStruct(q.shape, BF16),
        grid=(n_batch, C_HEADS, nq),
        in_specs=[pl.BlockSpec((tq, hw), lambda b, h, i: (b * nq + i, h)),
                  pl.BlockSpec((None, kv_len, hw), lambda b, h, i: (b, kv_block, h)),
                  pl.BlockSpec((None, kv_len, hw), lambda b, h, i: (b, kv_block, h)),
                  pl.BlockSpec(lam_qk.shape, lambda b, h, i: (0, 0)),
                  pl.BlockSpec((1, hw), lambda b, h, i: (0, 0))],
        out_specs=pl.BlockSpec((tq, hw), lambda b, h, i: (b * nq + i, h)),
        scratch_shapes=_flash_scratch(2 * tq, tk, hw),
        compiler_params=_params("arbitrary", "arbitrary", "arbitrary"),
        name="attn_diff",
    )(q, k, v, lam_qk, g_subln.reshape(1, hw))


def _conv3(c, c_prev_row, c_next_row, w_ref):
    tm = c.shape[0]
    row = lax.broadcasted_iota(jnp.int32, (tm, 1), 0)
    prev = jnp.where(row == 0, c_prev_row, pltpu.roll(c, 1, 0))
    nxt = jnp.where(row == tm - 1, c_next_row, pltpu.roll(c, tm - 1, 0))
    return w_ref[0:1, :] * prev + w_ref[1:2, :] * c + w_ref[2:3, :] * nxt


def _halo_rows(prev_val, next_val, tiles_per_seq):
    ti = pl.program_id(0) % tiles_per_seq
    prev_row = jnp.where(ti == 0, 0.0, prev_val)
    next_row = jnp.where(ti == tiles_per_seq - 1, 0.0, next_val)
    return prev_row, next_row


def _shortconv_kernel(bb_ref, bc_ref, bx_ref, bcp_ref, bxp_ref, bcn_ref, bxn_ref, w_ref, o_ref,
                      *, tiles_per_seq):
    last = BF16_SUBLANES - 1
    c = bc_ref[...].astype(F32) * bx_ref[...].astype(F32)
    cp = bcp_ref[last:last + 1, :].astype(F32) * bxp_ref[last:last + 1, :].astype(F32)
    cn = bcn_ref[0:1, :].astype(F32) * bxn_ref[0:1, :].astype(F32)
    cp, cn = _halo_rows(cp, cn, tiles_per_seq)
    y = bb_ref[...].astype(F32) * _conv3(c, cp, cn, w_ref)
    o_ref[...] = y.astype(o_ref.dtype)


def _glu_kernel(u_ref, g_ref, gp_ref, gn_ref, w_ref, o_ref, *, tiles_per_seq):
    last = BF16_SUBLANES - 1
    g = g_ref[...].astype(F32)
    gp, gn = _halo_rows(gp_ref[last:last + 1, :].astype(F32), gn_ref[0:1, :].astype(F32),
                        tiles_per_seq)
    z = _conv3(g, gp, gn, w_ref)
    y = z * jax.nn.sigmoid(z) * u_ref[...].astype(F32)
    o_ref[...] = y.astype(o_ref.dtype)


def _conv_specs(m, tm, wb, col_block):
    hb = tm // BF16_SUBLANES
    n_hb = m // BF16_SUBLANES
    main = pl.BlockSpec((tm, wb), lambda i, j: (i, col_block + j))
    prev = pl.BlockSpec((BF16_SUBLANES, wb),
                        lambda i, j: (jnp.maximum(i * hb - 1, 0), col_block + j))
    nxt = pl.BlockSpec((BF16_SUBLANES, wb),
                       lambda i, j: (jnp.minimum((i + 1) * hb, n_hb - 1), col_block + j))
    return main, prev, nxt


def _shortconv(st, p, w_conv, bb_col, bc_col, bx_col):
    m = p.shape[0]
    width = w_conv.shape[1]
    tm = _tile(st.seq_len, 256, BF16_SUBLANES)
    wb = _tile(math.gcd(math.gcd(bb_col, bc_col), math.gcd(bx_col, width)), 1024, LANES)
    bb = pl.BlockSpec((tm, wb), lambda i, j: (i, bb_col // wb + j))
    bc, bcp, bcn = _conv_specs(m, tm, wb, bc_col // wb)
    bx, bxp, bxn = _conv_specs(m, tm, wb, bx_col // wb)
    return pl.pallas_call(
        functools.partial(_shortconv_kernel, tiles_per_seq=st.seq_len // tm),
        out_shape=jax.ShapeDtypeStruct((m, width), BF16),
        grid=(m // tm, width // wb),
        in_specs=[bb, bc, bx, bcp, bxp, bcn, bxn,
                  pl.BlockSpec((w_conv.shape[0], wb), lambda i, j: (0, j))],
        out_specs=pl.BlockSpec((tm, wb), lambda i, j: (i, j)),
        compiler_params=_params("arbitrary", "arbitrary"),
        name="shortconv",
    )(p, p, p, p, p, p, p, w_conv)


def _glu(st, ug, w_conv):
    m = ug.shape[0]
    d_ff = w_conv.shape[1]
    tm = _tile(st.seq_len, 256, BF16_SUBLANES)
    wb = _tile(d_ff, 1024, LANES)
    u = pl.BlockSpec((tm, wb), lambda i, j: (i, j))
    g, gp, gn = _conv_specs(m, tm, wb, d_ff // wb)
    return pl.pallas_call(
        functools.partial(_glu_kernel, tiles_per_seq=st.seq_len // tm),
        out_shape=jax.ShapeDtypeStruct((m, d_ff), BF16),
        grid=(m // tm, d_ff // wb),
        in_specs=[u, g, gp, gn, pl.BlockSpec((w_conv.shape[0], wb), lambda i, j: (0, j))],
        out_specs=pl.BlockSpec((tm, wb), lambda i, j: (i, j)),
        compiler_params=_params("arbitrary", "arbitrary"),
        name="glu",
    )(ug, ug, ug, ug, w_conv)


def _branch_kernel(ya_ref, yb_ref, yc_ref, w0_ref, w1_ref, w2_ref, g0_ref, g1_ref, g2_ref, o_ref):
    acc = jax.nn.sigmoid(g0_ref[...].astype(F32)) * jnp.dot(
        ya_ref[...], w0_ref[...], preferred_element_type=F32)
    acc += jax.nn.sigmoid(g1_ref[...].astype(F32)) * jnp.dot(
        yb_ref[...], w1_ref[...], preferred_element_type=F32)
    acc += jax.nn.sigmoid(g2_ref[...].astype(F32)) * jnp.dot(
        yc_ref[...], w2_ref[...], preferred_element_type=F32)
    o_ref[...] = acc.astype(o_ref.dtype)


def _branch_merge(ya, yb, yc, w_branch, p, gates_col):
    m, k = ya.shape
    d = w_branch.shape[2]
    tm = _tile(m, 512, BF16_SUBLANES)
    tn = _tile(math.gcd(gates_col, d), 512, LANES)
    a_spec = pl.BlockSpec((tm, k), lambda i, j: (i, 0))

    def w_spec(n):
        return pl.BlockSpec((None, k, tn), lambda i, j: (n, 0, j))

    def g_spec(n):
        return pl.BlockSpec((tm, tn), lambda i, j: (i, (gates_col + n * d) // tn + j))

    return pl.pallas_call(
        _branch_kernel,
        out_shape=jax.ShapeDtypeStruct((m, d), BF16),
        grid=(m // tm, d // tn),
        in_specs=[a_spec, a_spec, a_spec, w_spec(0), w_spec(1), w_spec(2),
                  g_spec(0), g_spec(1), g_spec(2)],
        out_specs=pl.BlockSpec((tm, tn), lambda i, j: (i, j)),
        compiler_params=_params("arbitrary", "arbitrary"),
        name="branch_merge",
    )(ya, yb, yc, w_branch, w_branch, w_branch, p, p, p)


def _mm_resid_kernel(a_ref, w_ref, x_ref, gate_ref, o_ref):
    y = jnp.dot(a_ref[...], w_ref[...], preferred_element_type=F32)
    o_ref[...] = x_ref[...] + gate_ref[...] * y


def _mm_resid(st, a, w, x, mod, gate_idx):
    m, k = a.shape
    d = w.shape[1]
    tm = _tile(st.rows_per_mod, 512, BF16_SUBLANES)
    tn = _tile(d, 512, LANES)
    tpm = st.rows_per_mod // tm
    gate_block = gate_idx * (d // tn)
    return pl.pallas_call(
        _mm_resid_kernel,
        out_shape=jax.ShapeDtypeStruct((m, d), F32),
        grid=(m // tm, d // tn),
        in_specs=[pl.BlockSpec((tm, k), lambda i, j: (i, 0)),
                  pl.BlockSpec((k, tn), lambda i, j: (0, j)),
                  pl.BlockSpec((tm, tn), lambda i, j: (i, j)),
                  pl.BlockSpec((None, 1, tn), lambda i, j: (i // tpm, 0, gate_block + j))],
        out_specs=pl.BlockSpec((tm, tn), lambda i, j: (i, j)),
        compiler_params=_params("arbitrary", "arbitrary"),
        name="mm_resid",
    )(a, w, x, mod)


def _rope_tables(n_tok):
    rows = n_tok // GRID_W
    row = jnp.broadcast_to(jnp.arange(rows, dtype=F32)[:, None], (rows, GRID_W)).reshape(-1)
    col = jnp.broadcast_to(jnp.arange(GRID_W, dtype=F32)[None, :], (rows, GRID_W)).reshape(-1)
    quarter = HEAD_DIM // 4
    freqs = ROPE_THETA ** (-jnp.arange(quarter, dtype=F32) / quarter)
    ang_r = row[:, None] * freqs
    ang_c = col[:, None] * freqs
    ang = jnp.concatenate([ang_r, ang_r, ang_c, ang_c], axis=-1)
    lane = jnp.arange(HEAD_DIM)
    sign = jnp.where((lane % (HEAD_DIM // 2)) < quarter, -1.0, 1.0).astype(F32)
    return jnp.cos(ang), jnp.sin(ang) * sign


def kernel(x, c, ctx, c_ctx, w_mod, b_mod, g_norm_mix, g_norm_ffn, w_in, g_qa, g_ka, g_qc, g_kc,
           w_conv_b, lam_qk, g_subln, w_branch, w_out, w_up, w_conv_ffn, w_down):
    n_batch, n_tok, d = x.shape
    ctx_len = ctx.shape[1]
    depth = w_in.shape[0]
    d_ff = w_down.shape[1]
    a_q_dim = A_Q_HEADS * HEAD_DIM
    a_kv_dim = A_KV_HEADS * HEAD_DIM
    c_dim = C_HEADS * 2 * HEAD_DIM
    kv_cols = 2 * a_kv_dim + 2 * c_dim
    qk_cols = kv_cols + a_q_dim + c_dim
    bb_col, bc_col, bx_col = qk_cols, qk_cols + B_WIDTH, qk_cols + 2 * B_WIDTH
    gates_col = qk_cols + 3 * B_WIDTH
    kv_tok = n_tok + ctx_len
    assert n_tok % ctx_len == 0, "context keys must form one key block of the combined key array"

    lat = _Stream(n_batch, n_tok, n_tok)
    con = _Stream(n_batch, ctx_len, n_batch * ctx_len)

    mod_rows = 8
    cvec = jnp.zeros((mod_rows, d), F32).at[:n_batch].set(c).at[n_batch].set(c_ctx)
    mod_all = _modvec(cvec, w_mod, b_mod)
    rope = _rope_tables(n_tok)

    q_scale = HEAD_DIM ** -0.5 * math.log2(math.e)
    ak_col, av_col, ck_col, cv_col = 0, a_kv_dim, 2 * a_kv_dim, 2 * a_kv_dim + c_dim
    aq_col, cq_col = kv_cols, kv_cols + a_q_dim
    gk_a, gk_c, gq_a, gq_c = 0, a_kv_dim, a_kv_dim + c_dim, a_kv_dim + c_dim + a_q_dim
    kv_segs = (("norm", ak_col, a_kv_dim, 0, gk_a, 1.0), ("copy", av_col, a_kv_dim, 1, 0, 1.0),
               ("norm", ck_col, c_dim, 2, gk_c, 1.0), ("copy", cv_col, c_dim, 3, 0, 1.0))
    q_segs = (("norm", aq_col, a_q_dim, 4, gq_a, q_scale), ("norm", cq_col, c_dim, 5, gq_c, q_scale))
    kv_widths = (a_kv_dim, a_kv_dim, c_dim, c_dim)

    xs = x.reshape(n_batch * n_tok, d)
    cs = ctx.reshape(n_batch * ctx_len, d)
    for l in range(depth):
        last = l == depth - 1
        lambda_init = 0.8 - 0.6 * math.exp(-0.3 * l)
        mod = mod_all[l]
        mod_x = mod[:n_batch].reshape(n_batch, 1, N_MOD * d)
        mod_c = mod[n_batch:n_batch + 1].reshape(1, 1, N_MOD * d)
        w_in_l = w_in[l].astype(BF16)
        w_branch_l = w_branch[l].astype(BF16)
        w_out_l = w_out[l].astype(BF16)
        w_up_l = w_up[l].astype(BF16)
        w_down_l = w_down[l].astype(BF16)
        gains = jnp.concatenate([
            jnp.tile(g_ka[l], A_KV_HEADS), jnp.tile(g_kc[l].reshape(-1), C_HEADS),
            jnp.tile(g_qa[l], A_Q_HEADS), jnp.tile(g_qc[l].reshape(-1), C_HEADS)]).reshape(1, -1)

        h_x = _norm_mod(lat, xs, g_norm_mix[l], mod_x, 0, 1)
        h_c = _norm_mod(con, cs, g_norm_mix[l], mod_c, 0, 1)
        p_x = _mm(h_x, w_in_l)
        p_c = _mm(h_c, w_in_l, n_cols=kv_cols if last else None)

        kv_shapes = [(None, ((n_batch, kv_tok, w), w)) for w in kv_widths]
        outs = _prep(lat, p_x, gains, kv_segs + q_segs, qk_cols, rope, kv_shapes, 0,
                     (a_q_dim, c_dim))
        ka, va, kc, vc, qa_x, qc_x = outs
        kv_arrs = [(arr, ((n_batch, kv_tok, w), w)) for arr, w in zip((ka, va, kc, vc), kv_widths)]
        if last:
            ka, va, kc, vc = _prep(con, p_c, gains, kv_segs, kv_cols, None, kv_arrs,
                                   n_tok // _tile(ctx_len, 256, BF16_SUBLANES), ())
        else:
            ka, va, kc, vc, qa_c, qc_c = _prep(
                con, p_c, gains, kv_segs + q_segs, qk_cols, None, kv_arrs,
                n_tok // _tile(ctx_len, 256, BF16_SUBLANES), (a_q_dim, c_dim))

        def mixer(st, p, qa, qc, x_in, mod_s, q_len, kv_len, kv_block):
            ya = _attn_a(qa, ka, va, n_batch, q_len, kv_len, kv_block)
            yc = _attn_c(qc, kc, vc, lam_qk[l], g_subln[l], lambda_init, n_batch, q_len, kv_len,
                         kv_block)
            yb = _shortconv(st, p, w_conv_b[l], bb_col, bc_col, bx_col)
            merged = _branch_merge(ya, yb, yc, w_branch_l, p, gates_col)
            return _mm_resid(st, merged, w_out_l, x_in, mod_s, 2)

        def channel_mixer(st, x_in, mod_s):
            f = _norm_mod(st, x_in, g_norm_ffn[l], mod_s, 3, 4)
            ug = _mm(f, w_up_l)
            act = _glu(st, ug, w_conv_ffn[l])
            return _mm_resid(st, act, w_down_l, x_in, mod_s, 5)

        xs = mixer(lat, p_x, qa_x, qc_x, xs, mod_x, n_tok, kv_tok, 0)
        if not last:
            cs = mixer(con, p_c, qa_c, qc_c, cs, mod_c, ctx_len, ctx_len, n_tok // ctx_len)
        xs = channel_mixer(lat, xs, mod_x)
        if not last:
            cs = channel_mixer(con, cs, mod_c)
    return xs.reshape(n_batch, n_tok, d)
```

```python
import functools
import math

import jax
import jax.numpy as jnp
from jax import lax
from jax.experimental import pallas as pl
from jax.experimental.pallas import tpu as pltpu

GRID_W = 64
HEAD_DIM = 128
ROPE_THETA = 10000.0
EPS = 1e-6
N_MOD = 6
A_Q_HEADS = 16
A_KV_HEADS = 4
B_WIDTH = 2048
C_HEADS = 8
N_BRANCH = 3

LANES = 128
BF16_SUBLANES = 16
VMEM_LIMIT = 56 << 20
ROW_TILE = 512

F32 = jnp.float32
BF16 = jnp.bfloat16


def _tile(n, target, mult):
    best = None
    t = mult
    while t <= min(n, target):
        if n % t == 0:
            best = t
        t += mult
    return n if best is None else best


def _params(*semantics):
    return pltpu.CompilerParams(dimension_semantics=semantics, vmem_limit_bytes=VMEM_LIMIT)


class _Stream:
    def __init__(self, n_seq, seq_len, rows_per_mod):
        self.n_seq, self.seq_len, self.rows_per_mod = n_seq, seq_len, rows_per_mod
        self.rows = n_seq * seq_len


def _modvec_kernel(c_ref, w_ref, b_ref, o_ref):
    c = c_ref[...]
    s = c * jax.nn.sigmoid(c)
    o_ref[...] = jnp.dot(s.astype(BF16), w_ref[...].astype(BF16),
                         preferred_element_type=F32) + b_ref[...]


def _modvec(cvec, w_mod, b_mod):
    n_layer, d, n = w_mod.shape
    rows = cvec.shape[0]
    tn = _tile(n, 512, LANES)
    return pl.pallas_call(
        _modvec_kernel,
        out_shape=jax.ShapeDtypeStruct((n_layer, rows, n), F32),
        grid=(n_layer, n // tn),
        in_specs=[pl.BlockSpec((rows, d), lambda l, j: (0, 0)),
                  pl.BlockSpec((None, d, tn), lambda l, j: (l, 0, j)),
                  pl.BlockSpec((None, 1, tn), lambda l, j: (l, 0, j))],
        out_specs=pl.BlockSpec((None, rows, tn), lambda l, j: (l, 0, j)),
        compiler_params=_params("arbitrary", "arbitrary"),
        name="modvec",
    )(cvec, w_mod, b_mod.reshape(n_layer, 1, n))


def _norm_mod_kernel(x_ref, g_ref, sh_ref, sc_ref, o_ref):
    x = x_ref[...]
    ms = jnp.mean(x * x, axis=-1, keepdims=True)
    y = x * lax.rsqrt(ms + EPS) * g_ref[...]
    o_ref[...] = (y * (1.0 + sc_ref[...]) + sh_ref[...]).astype(o_ref.dtype)


def _norm_mod(st, x, g, mod, shift_idx, scale_idx):
    m, d = x.shape
    tm = _tile(math.gcd(st.seq_len, st.rows_per_mod), ROW_TILE, BF16_SUBLANES)
    tpm = st.rows_per_mod // tm
    return pl.pallas_call(
        _norm_mod_kernel,
        out_shape=jax.ShapeDtypeStruct((m, d), BF16),
        grid=(m // tm,),
        in_specs=[pl.BlockSpec((tm, d), lambda i: (i, 0)),
                  pl.BlockSpec((1, d), lambda i: (0, 0)),
                  pl.BlockSpec((None, 1, d), lambda i: (i // tpm, 0, shift_idx)),
                  pl.BlockSpec((None, 1, d), lambda i: (i // tpm, 0, scale_idx))],
        out_specs=pl.BlockSpec((tm, d), lambda i: (i, 0)),
        compiler_params=_params("arbitrary"),
        name="norm_mod",
    )(x, g.reshape(1, d), mod, mod)


def _mm_kernel(a_ref, w_ref, o_ref, wb_ref):
    @pl.when(pl.program_id(1) == 0)
    def _():
        wb_ref[...] = w_ref[...].astype(BF16)

    o_ref[...] = jnp.dot(a_ref[...], wb_ref[...], preferred_element_type=F32).astype(o_ref.dtype)


def _mm(a, w_stack, layer, n_cols=None, tm_target=1024, tn_target=512):
    m, k = a.shape
    n = w_stack.shape[2] if n_cols is None else n_cols
    tm = _tile(m, tm_target, BF16_SUBLANES)
    tn = _tile(n, tn_target, LANES)
    return pl.pallas_call(
        _mm_kernel,
        out_shape=jax.ShapeDtypeStruct((m, n), BF16),
        grid=(n // tn, m // tm),
        in_specs=[pl.BlockSpec((tm, k), lambda j, i: (i, 0)),
                  pl.BlockSpec((None, k, tn), lambda j, i: (layer, 0, j))],
        out_specs=pl.BlockSpec((tm, tn), lambda j, i: (i, j)),
        scratch_shapes=[pltpu.VMEM((k, tn), BF16)],
        compiler_params=_params("arbitrary", "arbitrary"),
        name="mm",
    )(a, w_stack)


def _norm_rope_head(x, g, cos, sin_signed, first_half, out_scale):
    ms = jnp.mean(x * x, axis=-1, keepdims=True)
    y = x * lax.rsqrt(ms + EPS) * g
    if cos is not None:
        rot = jnp.where(first_half, pltpu.roll(y, HEAD_DIM - HEAD_DIM // 4, 1),
                        pltpu.roll(y, HEAD_DIM // 4, 1))
        y = y * cos + rot * sin_signed
    if out_scale != 1.0:
        y = y * out_scale
    return y


def _prep_kernel(*refs, segs, rope):
    n_out = len(segs)
    p_ref, g_ref = refs[0], refs[1]
    if rope:
        cos = refs[2][...]
        sin_signed = refs[3][...]
        outs = refs[4:4 + n_out]
        lane = lax.broadcasted_iota(jnp.int32, (1, HEAD_DIM), 1)
        first_half = (lane % (HEAD_DIM // 2)) < (HEAD_DIM // 4)
    else:
        cos = sin_signed = first_half = None
        outs = refs[2:2 + n_out]
    for (kind, in_col, width, out_index, gain_col, out_scale) in segs:
        o_ref = outs[out_index]
        if kind == "copy":
            o_ref[...] = p_ref[:, in_col:in_col + width]
            continue
        for h in range(width // HEAD_DIM):
            x = p_ref[:, in_col + h * HEAD_DIM:in_col + (h + 1) * HEAD_DIM].astype(F32)
            g = g_ref[:, gain_col + h * HEAD_DIM:gain_col + (h + 1) * HEAD_DIM]
            y = _norm_rope_head(x, g, cos, sin_signed, first_half, out_scale)
            o_ref[:, h * HEAD_DIM:(h + 1) * HEAD_DIM] = y.astype(o_ref.dtype)


def _prep(st, p, gains, segs, in_width, rope_tables, kv_outs, kv_row_block, q_widths):
    m = p.shape[0]
    tm = _tile(st.seq_len, ROW_TILE, BF16_SUBLANES)
    tps = st.seq_len // tm
    rope = rope_tables is not None
    in_specs = [pl.BlockSpec((tm, in_width), lambda i: (i, 0)),
                pl.BlockSpec((1, gains.shape[1]), lambda i: (0, 0))]
    args = [p, gains]
    if rope:
        in_specs += [pl.BlockSpec((tm, HEAD_DIM), lambda i: (i % tps, 0))] * 2
        args += list(rope_tables)
    out_shapes, out_specs, aliases = [], [], {}
    for arr, (shape, width) in kv_outs:
        out_shapes.append(jax.ShapeDtypeStruct(shape, BF16))
        out_specs.append(pl.BlockSpec((None, tm, width),
                                      lambda i: (i // tps, kv_row_block + i % tps, 0)))
        if arr is not None:
            aliases[len(args)] = len(out_shapes) - 1
            in_specs.append(pl.BlockSpec(memory_space=pl.ANY))
            args.append(arr)
    for width in q_widths:
        out_shapes.append(jax.ShapeDtypeStruct((m, width), BF16))
        out_specs.append(pl.BlockSpec((tm, width), lambda i: (i, 0)))
    return pl.pallas_call(
        functools.partial(_prep_kernel_aliased, segs=segs, rope=rope, n_alias=len(aliases)),
        out_shape=out_shapes,
        grid=(m // tm,),
        in_specs=in_specs,
        out_specs=out_specs,
        input_output_aliases=aliases,
        compiler_params=_params("arbitrary"),
        name="prep",
    )(*args)


def _prep_kernel_aliased(*refs, segs, rope, n_alias):
    n_in = 2 + (2 if rope else 0)
    _prep_kernel(*refs[:n_in], *refs[n_in + n_alias:], segs=segs, rope=rope)


def _flash(q_ref, q_parts, k_ref, v_ref, tk, n_chunks, s_ref, m_ref, l_ref, acc_ref):
    m_ref[...] = jnp.full(m_ref.shape, -jnp.inf, F32)
    l_ref[...] = jnp.zeros(l_ref.shape, F32)
    acc_ref[...] = jnp.zeros(acc_ref.shape, F32)

    def scores(c, slot):
        start = c * tk
        for row0, n_rows, k_col in q_parts:
            k = k_ref[pl.ds(start, tk), k_col:k_col + HEAD_DIM]
            s_ref[slot, row0:row0 + n_rows, :] = lax.dot_general(
                q_ref[row0:row0 + n_rows, :], k, (((1,), (1,)), ((), ())),
                preferred_element_type=F32)

    n_col = tk // LANES
    v_rep = acc_ref.shape[1] // LANES

    def update(c, slot):
        start = c * tk
        s_blocks = [s_ref[slot, :, j * LANES:(j + 1) * LANES] for j in range(n_col)]
        m_cur = s_blocks[0]
        for sb in s_blocks[1:]:
            m_cur = jnp.maximum(m_cur, sb)
        m_prev = m_ref[...]
        m_new = jnp.maximum(m_prev, jnp.max(m_cur, axis=-1, keepdims=True))
        alpha = jnp.exp2(m_prev - m_new)
        p_blocks = [jnp.exp2(sb - m_new) for sb in s_blocks]
        l_new = alpha * l_ref[...]
        for pb in p_blocks:
            l_new = l_new + pb
        l_ref[...] = l_new
        p = jnp.concatenate([pb.astype(BF16) for pb in p_blocks], axis=1)
        v = v_ref[pl.ds(start, tk), :]
        alpha_acc = alpha if v_rep == 1 else jnp.concatenate([alpha] * v_rep, axis=1)
        acc_ref[...] = alpha_acc * acc_ref[...] + jnp.dot(p, v, preferred_element_type=F32)
        m_ref[...] = m_new

    scores(0, 0)
    for c in range(n_chunks):
        if c + 1 < n_chunks:
            scores(c + 1, (c + 1) % 2)
        update(c, c % 2)


def _attn_a_kernel(q_ref, k_ref, v_ref, o_ref, qs_ref, s_ref, m_ref, l_ref, acc_ref,
                   *, group, tk, n_chunks):
    tq = q_ref.shape[0]
    for g in range(group):
        qs_ref[g * tq:(g + 1) * tq, :] = q_ref[:, g * HEAD_DIM:(g + 1) * HEAD_DIM]
    _flash(qs_ref, ((0, group * tq, 0),), k_ref, v_ref, tk, n_chunks, s_ref, m_ref, l_ref, acc_ref)
    for g in range(group):
        rows = slice(g * tq, (g + 1) * tq)
        l = jnp.sum(l_ref[rows, :], axis=-1, keepdims=True)
        o_ref[:, g * HEAD_DIM:(g + 1) * HEAD_DIM] = (acc_ref[rows, :] / l).astype(o_ref.dtype)


def _attn_c_kernel(q_ref, k_ref, v_ref, lam_ref, g_ref, o_ref, qs_ref, s_ref, m_ref, l_ref,
                   acc_ref, *, tk, n_chunks, lambda_init):
    tq = q_ref.shape[0]
    for j in range(2):
        qs_ref[j * tq:(j + 1) * tq, :] = q_ref[:, j * HEAD_DIM:(j + 1) * HEAD_DIM]
    _flash(qs_ref, ((0, tq, 0), (tq, tq, HEAD_DIM)), k_ref, v_ref, tk, n_chunks, s_ref, m_ref,
           l_ref, acc_ref)
    lq = lam_ref[...]
    lam = (jnp.exp(jnp.sum(lq[0:1] * lq[1:2], axis=-1, keepdims=True))
           - jnp.exp(jnp.sum(lq[2:3] * lq[3:4], axis=-1, keepdims=True)) + lambda_init)
    l0 = jnp.sum(l_ref[0:tq, :], axis=-1, keepdims=True)
    l1 = jnp.sum(l_ref[tq:2 * tq, :], axis=-1, keepdims=True)
    y = (acc_ref[0:tq, :] / l0) - lam * (acc_ref[tq:2 * tq, :] / l1)
    ms = jnp.mean(y * y, axis=-1, keepdims=True)
    y = y * lax.rsqrt(ms + EPS) * g_ref[...]
    o_ref[...] = (y * (1.0 - lambda_init)).astype(o_ref.dtype)


def _kv_chunk(kv_len):
    return _tile(kv_len, 1024, 2 * LANES)


def _flash_scratch(rows, tk, v_width):
    return [pltpu.VMEM((rows, HEAD_DIM), BF16), pltpu.VMEM((2, rows, tk), F32),
            pltpu.VMEM((rows, LANES), F32), pltpu.VMEM((rows, LANES), F32),
            pltpu.VMEM((rows, v_width), F32)]


def _attn_a(q, k, v, n_batch, q_len, kv_len, kv_block):
    group = A_Q_HEADS // A_KV_HEADS
    gw = group * HEAD_DIM
    tq = _tile(q_len, 1024 // group, BF16_SUBLANES)
    nq = q_len // tq
    tk = _kv_chunk(kv_len)
    kernel = functools.partial(_attn_a_kernel, group=group, tk=tk, n_chunks=kv_len // tk)
    return pl.pallas_call(
        kernel,
        out_shape=jax.ShapeDtypeStruct(q.shape, BF16),
        grid=(n_batch, A_KV_HEADS, nq),
        in_specs=[pl.BlockSpec((tq, gw), lambda b, h, i: (b * nq + i, h)),
                  pl.BlockSpec((None, kv_len, HEAD_DIM), lambda b, h, i: (b, kv_block, h)),
                  pl.BlockSpec((None, kv_len, HEAD_DIM), lambda b, h, i: (b, kv_block, h))],
        out_specs=pl.BlockSpec((tq, gw), lambda b, h, i: (b * nq + i, h)),
        scratch_shapes=_flash_scratch(group * tq, tk, HEAD_DIM),
        compiler_params=_params("arbitrary", "arbitrary", "arbitrary"),
        name="attn_gqa",
    )(q, k, v)


def _attn_c(q, k, v, lam_qk, g_subln, lambda_init, n_batch, q_len, kv_len, kv_block):
    hw = 2 * HEAD_DIM
    tq = _tile(q_len, 512, BF16_SUBLANES)
    nq = q_len // tq
    tk = _kv_chunk(kv_len)
    kernel = functools.partial(_attn_c_kernel, tk=tk, n_chunks=kv_len // tk,
                               lambda_init=lambda_init)
    return pl.pallas_call(
        kernel,
        out_shape=jax.ShapeDtypeStruct(q.shape, BF16),
        grid=(n_batch, C_HEADS, nq),
        in_specs=[pl.BlockSpec((tq, hw), lambda b, h, i: (b * nq + i, h)),
                  pl.BlockSpec((None, kv_len, hw), lambda b, h, i: (b, kv_block, h)),
                  pl.BlockSpec((None, kv_len, hw), lambda b, h, i: (b, kv_block, h)),
                  pl.BlockSpec(lam_qk.shape, lambda b, h, i: (0, 0)),
                  pl.BlockSpec((1, hw), lambda b, h, i: (0, 0))],
        out_specs=pl.BlockSpec((tq, hw), lambda b, h, i: (b * nq + i, h)),
        scratch_shapes=_flash_scratch(2 * tq, tk, hw),
        compiler_params=_params("arbitrary", "arbitrary", "arbitrary"),
        name="attn_diff",
    )(q, k, v, lam_qk, g_subln.reshape(1, hw))


def _conv3(c, c_prev_row, c_next_row, w_ref):
    tm = c.shape[0]
    row = lax.broadcasted_iota(jnp.int32, (tm, 1), 0)
    prev = jnp.where(row == 0, c_prev_row, pltpu.roll(c, 1, 0))
    nxt = jnp.where(row == tm - 1, c_next_row, pltpu.roll(c, tm - 1, 0))
    return w_ref[0:1, :] * prev + w_ref[1:2, :] * c + w_ref[2:3, :] * nxt


def _halo_rows(prev_val, next_val, tiles_per_seq):
    ti = pl.program_id(0) % tiles_per_seq
    prev_row = jnp.where(ti == 0, 0.0, prev_val)
    next_row = jnp.where(ti == tiles_per_seq - 1, 0.0, next_val)
    return prev_row, next_row


def _shortconv_kernel(bb_ref, bc_ref, bx_ref, bcp_ref, bxp_ref, bcn_ref, bxn_ref, w_ref, o_ref,
                      *, tiles_per_seq):
    last = BF16_SUBLANES - 1
    c = bc_ref[...].astype(F32) * bx_ref[...].astype(F32)
    cp = bcp_ref[last:last + 1, :].astype(F32) * bxp_ref[last:last + 1, :].astype(F32)
    cn = bcn_ref[0:1, :].astype(F32) * bxn_ref[0:1, :].astype(F32)
    cp, cn = _halo_rows(cp, cn, tiles_per_seq)
    y = bb_ref[...].astype(F32) * _conv3(c, cp, cn, w_ref)
    o_ref[...] = y.astype(o_ref.dtype)


def _glu_kernel(u_ref, g_ref, gp_ref, gn_ref, w_ref, o_ref, *, tiles_per_seq):
    last = BF16_SUBLANES - 1
    g = g_ref[...].astype(F32)
    gp, gn = _halo_rows(gp_ref[last:last + 1, :].astype(F32), gn_ref[0:1, :].astype(F32),
                        tiles_per_seq)
    z = _conv3(g, gp, gn, w_ref)
    y = z * jax.nn.sigmoid(z) * u_ref[...].astype(F32)
    o_ref[...] = y.astype(o_ref.dtype)


def _conv_specs(m, tm, wb, col_block):
    hb = tm // BF16_SUBLANES
    n_hb = m // BF16_SUBLANES
    main = pl.BlockSpec((tm, wb), lambda i, j: (i, col_block + j))
    prev = pl.BlockSpec((BF16_SUBLANES, wb),
                        lambda i, j: (jnp.maximum(i * hb - 1, 0), col_block + j))
    nxt = pl.BlockSpec((BF16_SUBLANES, wb),
                       lambda i, j: (jnp.minimum((i + 1) * hb, n_hb - 1), col_block + j))
    return main, prev, nxt


def _shortconv(st, p, w_conv, bb_col, bc_col, bx_col):
    m = p.shape[0]
    width = w_conv.shape[1]
    tm = _tile(st.seq_len, ROW_TILE, BF16_SUBLANES)
    wb = _tile(math.gcd(math.gcd(bb_col, bc_col), math.gcd(bx_col, width)), 1024, LANES)
    bb = pl.BlockSpec((tm, wb), lambda i, j: (i, bb_col // wb + j))
    bc, bcp, bcn = _conv_specs(m, tm, wb, bc_col // wb)
    bx, bxp, bxn = _conv_specs(m, tm, wb, bx_col // wb)
    return pl.pallas_call(
        functools.partial(_shortconv_kernel, tiles_per_seq=st.seq_len // tm),
        out_shape=jax.ShapeDtypeStruct((m, width), BF16),
        grid=(m // tm, width // wb),
        in_specs=[bb, bc, bx, bcp, bxp, bcn, bxn,
                  pl.BlockSpec((w_conv.shape[0], wb), lambda i, j: (0, j))],
        out_specs=pl.BlockSpec((tm, wb), lambda i, j: (i, j)),
        compiler_params=_params("arbitrary", "arbitrary"),
        name="shortconv",
    )(p, p, p, p, p, p, p, w_conv)


def _glu(st, ug, w_conv):
    m = ug.shape[0]
    d_ff = w_conv.shape[1]
    tm = _tile(st.seq_len, ROW_TILE, BF16_SUBLANES)
    wb = _tile(d_ff, 2048, LANES)
    u = pl.BlockSpec((tm, wb), lambda i, j: (i, j))
    g, gp, gn = _conv_specs(m, tm, wb, d_ff // wb)
    return pl.pallas_call(
        functools.partial(_glu_kernel, tiles_per_seq=st.seq_len // tm),
        out_shape=jax.ShapeDtypeStruct((m, d_ff), BF16),
        grid=(m // tm, d_ff // wb),
        in_specs=[u, g, gp, gn, pl.BlockSpec((w_conv.shape[0], wb), lambda i, j: (0, j))],
        out_specs=pl.BlockSpec((tm, wb), lambda i, j: (i, j)),
        compiler_params=_params("arbitrary", "arbitrary"),
        name="glu",
    )(ug, ug, ug, ug, w_conv)


def _branch_kernel(ya_ref, yb_ref, yc_ref, w0_ref, w1_ref, w2_ref, g0_ref, g1_ref, g2_ref, o_ref):
    acc = jax.nn.sigmoid(g0_ref[...].astype(F32)) * jnp.dot(
        ya_ref[...], w0_ref[...], preferred_element_type=F32)
    acc += jax.nn.sigmoid(g1_ref[...].astype(F32)) * jnp.dot(
        yb_ref[...], w1_ref[...], preferred_element_type=F32)
    acc += jax.nn.sigmoid(g2_ref[...].astype(F32)) * jnp.dot(
        yc_ref[...], w2_ref[...], preferred_element_type=F32)
    o_ref[...] = acc.astype(o_ref.dtype)


def _branch_merge(ya, yb, yc, w_branch, layer, p, gates_col):
    m, k = ya.shape
    d = w_branch.shape[3]
    tm = _tile(m, 1024, BF16_SUBLANES)
    tn = _tile(math.gcd(gates_col, d), 512, LANES)
    a_spec = pl.BlockSpec((tm, k), lambda i, j: (i, 0))

    def w_spec(n):
        return pl.BlockSpec((None, None, k, tn), lambda i, j: (layer, n, 0, j))

    def g_spec(n):
        return pl.BlockSpec((tm, tn), lambda i, j: (i, (gates_col + n * d) // tn + j))

    return pl.pallas_call(
        _branch_kernel,
        out_shape=jax.ShapeDtypeStruct((m, d), BF16),
        grid=(m // tm, d // tn),
        in_specs=[a_spec, a_spec, a_spec, w_spec(0), w_spec(1), w_spec(2),
                  g_spec(0), g_spec(1), g_spec(2)],
        out_specs=pl.BlockSpec((tm, tn), lambda i, j: (i, j)),
        compiler_params=_params("arbitrary", "arbitrary"),
        name="branch_merge",
    )(ya, yb, yc, w_branch, w_branch, w_branch, p, p, p)


def _mm_resid_kernel(a_ref, w_ref, x_ref, gate_ref, o_ref):
    y = jnp.dot(a_ref[...], w_ref[...], preferred_element_type=F32)
    o_ref[...] = x_ref[...] + gate_ref[...] * y


def _mm_resid(st, a, w_stack, layer, x, mod, gate_idx):
    m, k = a.shape
    d = w_stack.shape[2]
    tm = _tile(st.rows_per_mod, 1024, BF16_SUBLANES)
    tn = _tile(d, 512 if k <= 4096 else 256, LANES)
    tpm = st.rows_per_mod // tm
    gate_block = gate_idx * (d // tn)
    return pl.pallas_call(
        _mm_resid_kernel,
        out_shape=jax.ShapeDtypeStruct((m, d), F32),
        grid=(m // tm, d // tn),
        in_specs=[pl.BlockSpec((tm, k), lambda i, j: (i, 0)),
                  pl.BlockSpec((None, k, tn), lambda i, j: (layer, 0, j)),
                  pl.BlockSpec((tm, tn), lambda i, j: (i, j)),
                  pl.BlockSpec((None, 1, tn), lambda i, j: (i // tpm, 0, gate_block + j))],
        out_specs=pl.BlockSpec((tm, tn), lambda i, j: (i, j)),
        compiler_params=_params("arbitrary", "arbitrary"),
        name="mm_resid",
    )(a, w_stack, x, mod)


def _rope_tables(n_tok):
    rows = n_tok // GRID_W
    row = jnp.broadcast_to(jnp.arange(rows, dtype=F32)[:, None], (rows, GRID_W)).reshape(-1)
    col = jnp.broadcast_to(jnp.arange(GRID_W, dtype=F32)[None, :], (rows, GRID_W)).reshape(-1)
    quarter = HEAD_DIM // 4
    freqs = ROPE_THETA ** (-jnp.arange(quarter, dtype=F32) / quarter)
    ang_r = row[:, None] * freqs
    ang_c = col[:, None] * freqs
    ang = jnp.concatenate([ang_r, ang_r, ang_c, ang_c], axis=-1)
    lane = jnp.arange(HEAD_DIM)
    sign = jnp.where((lane % (HEAD_DIM // 2)) < quarter, -1.0, 1.0).astype(F32)
    return jnp.cos(ang), jnp.sin(ang) * sign


def kernel(x, c, ctx, c_ctx, w_mod, b_mod, g_norm_mix, g_norm_ffn, w_in, g_qa, g_ka, g_qc, g_kc,
           w_conv_b, lam_qk, g_subln, w_branch, w_out, w_up, w_conv_ffn, w_down):
    n_batch, n_tok, d = x.shape
    ctx_len = ctx.shape[1]
    depth = w_in.shape[0]
    a_q_dim = A_Q_HEADS * HEAD_DIM
    a_kv_dim = A_KV_HEADS * HEAD_DIM
    c_dim = C_HEADS * 2 * HEAD_DIM
    kv_cols = 2 * a_kv_dim + 2 * c_dim
    qk_cols = kv_cols + a_q_dim + c_dim
    bb_col, bc_col, bx_col = qk_cols, qk_cols + B_WIDTH, qk_cols + 2 * B_WIDTH
    gates_col = qk_cols + 3 * B_WIDTH
    kv_tok = n_tok + ctx_len
    assert n_tok % ctx_len == 0, "context keys must form one key block of the combined key array"

    lat = _Stream(n_batch, n_tok, n_tok)
    con = _Stream(n_batch, ctx_len, n_batch * ctx_len)

    mod_rows = 8
    cvec = jnp.zeros((mod_rows, d), F32).at[:n_batch].set(c).at[n_batch].set(c_ctx)
    mod_all = _modvec(cvec, w_mod, b_mod)
    rope = _rope_tables(n_tok)

    q_scale = HEAD_DIM ** -0.5 * math.log2(math.e)
    ak_col, av_col, ck_col, cv_col = 0, a_kv_dim, 2 * a_kv_dim, 2 * a_kv_dim + c_dim
    aq_col, cq_col = kv_cols, kv_cols + a_q_dim
    gk_a, gk_c, gq_a, gq_c = 0, a_kv_dim, a_kv_dim + c_dim, a_kv_dim + c_dim + a_q_dim
    kv_segs = (("norm", ak_col, a_kv_dim, 0, gk_a, 1.0), ("copy", av_col, a_kv_dim, 1, 0, 1.0),
               ("norm", ck_col, c_dim, 2, gk_c, 1.0), ("copy", cv_col, c_dim, 3, 0, 1.0))
    q_segs = (("norm", aq_col, a_q_dim, 4, gq_a, q_scale), ("norm", cq_col, c_dim, 5, gq_c, q_scale))
    kv_widths = (a_kv_dim, a_kv_dim, c_dim, c_dim)

    w_branch_b = w_branch.astype(BF16)
    w_out_b = w_out.astype(BF16)
    w_down_b = w_down.astype(BF16)

    xs = x.reshape(n_batch * n_tok, d)
    cs = ctx.reshape(n_batch * ctx_len, d)
    for l in range(depth):
        last = l == depth - 1
        lambda_init = 0.8 - 0.6 * math.exp(-0.3 * l)
        mod = mod_all[l]
        mod_x = mod[:n_batch].reshape(n_batch, 1, N_MOD * d)
        mod_c = mod[n_batch:n_batch + 1].reshape(1, 1, N_MOD * d)
        gains = jnp.concatenate([
            jnp.tile(g_ka[l], A_KV_HEADS), jnp.tile(g_kc[l].reshape(-1), C_HEADS),
            jnp.tile(g_qa[l], A_Q_HEADS), jnp.tile(g_qc[l].reshape(-1), C_HEADS)]).reshape(1, -1)

        h_x = _norm_mod(lat, xs, g_norm_mix[l], mod_x, 0, 1)
        h_c = _norm_mod(con, cs, g_norm_mix[l], mod_c, 0, 1)
        p_x = _mm(h_x, w_in, l)
        p_c = _mm(h_c, w_in, l, n_cols=kv_cols if last else None)

        kv_shapes = [(None, ((n_batch, kv_tok, w), w)) for w in kv_widths]
        outs = _prep(lat, p_x, gains, kv_segs + q_segs, qk_cols, rope, kv_shapes, 0,
                     (a_q_dim, c_dim))
        ka, va, kc, vc, qa_x, qc_x = outs
        kv_arrs = [(arr, ((n_batch, kv_tok, w), w)) for arr, w in zip((ka, va, kc, vc), kv_widths)]
        if last:
            ka, va, kc, vc = _prep(con, p_c, gains, kv_segs, kv_cols, None, kv_arrs,
                                   n_tok // _tile(ctx_len, ROW_TILE, BF16_SUBLANES), ())
        else:
            ka, va, kc, vc, qa_c, qc_c = _prep(
                con, p_c, gains, kv_segs + q_segs, qk_cols, None, kv_arrs,
                n_tok // _tile(ctx_len, ROW_TILE, BF16_SUBLANES), (a_q_dim, c_dim))

        def mixer(st, p, qa, qc, x_in, mod_s, q_len, kv_len, kv_block):
            ya = _attn_a(qa, ka, va, n_batch, q_len, kv_len, kv_block)
            yc = _attn_c(qc, kc, vc, lam_qk[l], g_subln[l], lambda_init, n_batch, q_len, kv_len,
                         kv_block)
            yb = _shortconv(st, p, w_conv_b[l], bb_col, bc_col, bx_col)
            merged = _branch_merge(ya, yb, yc, w_branch_b, l, p, gates_col)
            return _mm_resid(st, merged, w_out_b, l, x_in, mod_s, 2)

        def channel_mixer(st, x_in, mod_s):
            f = _norm_mod(st, x_in, g_norm_ffn[l], mod_s, 3, 4)
            ug = _mm(f, w_up, l)
            act = _glu(st, ug, w_conv_ffn[l])
            return _mm_resid(st, act, w_down_b, l, x_in, mod_s, 5)

        xs = mixer(lat, p_x, qa_x, qc_x, xs, mod_x, n_tok, kv_tok, 0)
        if not last:
            cs = mixer(con, p_c, qa_c, qc_c, cs, mod_c, ctx_len, ctx_len, n_tok // ctx_len)
        xs = channel_mixer(lat, xs, mod_x)
        if not last:
            cs = channel_mixer(con, cs, mod_c)
    return xs.reshape(n_batch, n_tok, d)
```

```python
import functools
import math

import jax
import jax.numpy as jnp
from jax import lax
from jax.experimental import pallas as pl
from jax.experimental.pallas import tpu as pltpu

GRID_W = 64
HEAD_DIM = 128
ROPE_THETA = 10000.0
EPS = 1e-6
N_MOD = 6
A_Q_HEADS = 16
A_KV_HEADS = 4
B_WIDTH = 2048
C_HEADS = 8
N_BRANCH = 3

LANES = 128
BF16_SUBLANES = 16
VMEM_LIMIT = 56 << 20
ROW_TILE = 512
F32 = jnp.float32
BF16 = jnp.bfloat16


def _tile(n, target, mult):
    best = None
    t = mult
    while t <= min(n, target):
        if n % t == 0:
            best = t
        t += mult
    return n if best is None else best


def _params(*semantics):
    return pltpu.CompilerParams(dimension_semantics=semantics, vmem_limit_bytes=VMEM_LIMIT)


class _Stream:
    def __init__(self, n_seq, seq_len, rows_per_mod):
        self.n_seq, self.seq_len, self.rows_per_mod = n_seq, seq_len, rows_per_mod
        self.rows = n_seq * seq_len


def _modvec_kernel(c_ref, w_ref, b_ref, o_ref):
    c = c_ref[...]
    s = c * jax.nn.sigmoid(c)
    o_ref[...] = jnp.dot(s.astype(BF16), w_ref[...].astype(BF16),
                         preferred_element_type=F32) + b_ref[...]


def _modvec(cvec, w_mod, b_mod):
    n_layer, d, n = w_mod.shape
    rows = cvec.shape[0]
    tn = _tile(n, 512, LANES)
    return pl.pallas_call(
        _modvec_kernel,
        out_shape=jax.ShapeDtypeStruct((n_layer, rows, n), F32),
        grid=(n_layer, n // tn),
        in_specs=[pl.BlockSpec((rows, d), lambda l, j: (0, 0)),
                  pl.BlockSpec((None, d, tn), lambda l, j: (l, 0, j)),
                  pl.BlockSpec((None, 1, tn), lambda l, j: (l, 0, j))],
        out_specs=pl.BlockSpec((None, rows, tn), lambda l, j: (l, 0, j)),
        compiler_params=_params("arbitrary", "arbitrary"),
        name="modvec",
    )(cvec, w_mod, b_mod.reshape(n_layer, 1, n))


def _norm_mod_kernel(x_ref, g_ref, sh_ref, sc_ref, o_ref):
    x = x_ref[...]
    ms = jnp.mean(x * x, axis=-1, keepdims=True)
    y = x * lax.rsqrt(ms + EPS) * g_ref[...]
    o_ref[...] = (y * (1.0 + sc_ref[...]) + sh_ref[...]).astype(o_ref.dtype)


def _norm_mod(st, x, g, mod, shift_idx, scale_idx):
    m, d = x.shape
    tm = _tile(math.gcd(st.seq_len, st.rows_per_mod), ROW_TILE, BF16_SUBLANES)
    tpm = st.rows_per_mod // tm
    return pl.pallas_call(
        _norm_mod_kernel,
        out_shape=jax.ShapeDtypeStruct((m, d), BF16),
        grid=(m // tm,),
        in_specs=[pl.BlockSpec((tm, d), lambda i: (i, 0)),
                  pl.BlockSpec((1, d), lambda i: (0, 0)),
                  pl.BlockSpec((None, 1, d), lambda i: (i // tpm, 0, shift_idx)),
                  pl.BlockSpec((None, 1, d), lambda i: (i // tpm, 0, scale_idx))],
        out_specs=pl.BlockSpec((tm, d), lambda i: (i, 0)),
        compiler_params=_params("arbitrary"),
        name="norm_mod",
    )(x, g.reshape(1, d), mod, mod)


def _mm_kernel(a_ref, w_ref, o_ref, wb_ref):
    @pl.when(pl.program_id(1) == 0)
    def _():
        wb_ref[...] = w_ref[...].astype(BF16)

    o_ref[...] = jnp.dot(a_ref[...], wb_ref[...], preferred_element_type=F32).astype(o_ref.dtype)


def _mm(a, w_stack, layer, n_cols=None, tm_target=1024, tn_target=512):
    m, k = a.shape
    n = w_stack.shape[2] if n_cols is None else n_cols
    tm = _tile(m, tm_target, BF16_SUBLANES)
    tn = _tile(n, tn_target, LANES)
    return pl.pallas_call(
        _mm_kernel,
        out_shape=jax.ShapeDtypeStruct((m, n), BF16),
        grid=(n // tn, m // tm),
        in_specs=[pl.BlockSpec((tm, k), lambda j, i: (i, 0)),
                  pl.BlockSpec((None, k, tn), lambda j, i: (layer, 0, j))],
        out_specs=pl.BlockSpec((tm, tn), lambda j, i: (i, j)),
        scratch_shapes=[pltpu.VMEM((k, tn), BF16)],
        compiler_params=_params("arbitrary", "arbitrary"),
        name="mm",
    )(a, w_stack)


def _lane_sum_mxu(x):
    hi = x.astype(BF16)
    lo = (x - hi.astype(F32)).astype(BF16)
    ones = jnp.ones((2 * x.shape[1], x.shape[1]), BF16)
    return jnp.dot(jnp.concatenate([hi, lo], axis=1), ones, preferred_element_type=F32)


def _norm_rope_head(x, g, cos, sin_signed, first_half, out_scale):
    ms = _lane_sum_mxu(x * x) * (1.0 / HEAD_DIM)
    y = x * lax.rsqrt(ms + EPS) * g
    if cos is not None:
        rot = jnp.where(first_half, pltpu.roll(y, HEAD_DIM - HEAD_DIM // 4, 1),
                        pltpu.roll(y, HEAD_DIM // 4, 1))
        y = y * cos + rot * sin_signed
    if out_scale != 1.0:
        y = y * out_scale
    return y


def _prep_kernel(*refs, segs, rope):
    n_out = len(segs)
    p_ref, g_ref = refs[0], refs[1]
    if rope:
        cos = refs[2][...]
        sin_signed = refs[3][...]
        outs = refs[4:4 + n_out]
        lane = lax.broadcasted_iota(jnp.int32, (1, HEAD_DIM), 1)
        first_half = (lane % (HEAD_DIM // 2)) < (HEAD_DIM // 4)
    else:
        cos = sin_signed = first_half = None
        outs = refs[2:2 + n_out]
    for (kind, in_col, width, out_index, gain_col, out_scale) in segs:
        o_ref = outs[out_index]
        if kind == "copy":
            o_ref[...] = p_ref[:, in_col:in_col + width]
            continue
        if kind == "copy_ones":
            for h in range(width // HEAD_DIM):
                o_ref[:, 2 * h * HEAD_DIM:(2 * h + 1) * HEAD_DIM] = p_ref[
                    :, in_col + h * HEAD_DIM:in_col + (h + 1) * HEAD_DIM]
                o_ref[:, (2 * h + 1) * HEAD_DIM:(2 * h + 2) * HEAD_DIM] = jnp.ones(
                    (o_ref.shape[0], HEAD_DIM), o_ref.dtype)
            continue
        for h in range(width // HEAD_DIM):
            x = p_ref[:, in_col + h * HEAD_DIM:in_col + (h + 1) * HEAD_DIM].astype(F32)
            g = g_ref[:, gain_col + h * HEAD_DIM:gain_col + (h + 1) * HEAD_DIM]
            y = _norm_rope_head(x, g, cos, sin_signed, first_half, out_scale)
            o_ref[:, h * HEAD_DIM:(h + 1) * HEAD_DIM] = y.astype(o_ref.dtype)


def _prep(st, p, gains, segs, in_width, rope_tables, kv_outs, kv_row_block, q_widths):
    m = p.shape[0]
    tm = _tile(st.seq_len, ROW_TILE, BF16_SUBLANES)
    tps = st.seq_len // tm
    rope = rope_tables is not None
    in_specs = [pl.BlockSpec((tm, in_width), lambda i: (i, 0)),
                pl.BlockSpec((1, gains.shape[1]), lambda i: (0, 0))]
    args = [p, gains]
    if rope:
        in_specs += [pl.BlockSpec((tm, HEAD_DIM), lambda i: (i % tps, 0))] * 2
        args += list(rope_tables)
    out_shapes, out_specs, aliases = [], [], {}
    for arr, (shape, width) in kv_outs:
        out_shapes.append(jax.ShapeDtypeStruct(shape, BF16))
        out_specs.append(pl.BlockSpec((None, tm, width),
                                      lambda i: (i // tps, kv_row_block + i % tps, 0)))
        if arr is not None:
            aliases[len(args)] = len(out_shapes) - 1
            in_specs.append(pl.BlockSpec(memory_space=pl.ANY))
            args.append(arr)
    for width in q_widths:
        out_shapes.append(jax.ShapeDtypeStruct((m, width), BF16))
        out_specs.append(pl.BlockSpec((tm, width), lambda i: (i, 0)))
    return pl.pallas_call(
        functools.partial(_prep_kernel_aliased, segs=segs, rope=rope, n_alias=len(aliases)),
        out_shape=out_shapes,
        grid=(m // tm,),
        in_specs=in_specs,
        out_specs=out_specs,
        input_output_aliases=aliases,
        compiler_params=_params("arbitrary"),
        name="prep",
    )(*args)


def _prep_kernel_aliased(*refs, segs, rope, n_alias):
    n_in = 2 + (2 if rope else 0)
    _prep_kernel(*refs[:n_in], *refs[n_in + n_alias:], segs=segs, rope=rope)


def _flash(q_ref, q_parts, k_ref, v_ref, tk, n_chunks, s_ref, m_ref, l_ref, acc_ref):
    m_ref[...] = jnp.full(m_ref.shape, -jnp.inf, F32)
    if l_ref is not None:
        l_ref[...] = jnp.zeros(l_ref.shape, F32)
    acc_ref[...] = jnp.zeros(acc_ref.shape, F32)

    def scores(c, slot):
        start = c * tk
        for row0, n_rows, k_col in q_parts:
            k = k_ref[pl.ds(start, tk), k_col:k_col + HEAD_DIM]
            s_ref[slot, row0:row0 + n_rows, :] = lax.dot_general(
                q_ref[row0:row0 + n_rows, :], k, (((1,), (1,)), ((), ())),
                preferred_element_type=F32)

    n_col = tk // LANES
    v_rep = acc_ref.shape[1] // LANES

    def update(c, slot):
        start = c * tk
        s_blocks = [s_ref[slot, :, j * LANES:(j + 1) * LANES] for j in range(n_col)]
        m_cur = s_blocks[0]
        for sb in s_blocks[1:]:
            m_cur = jnp.maximum(m_cur, sb)
        m_prev = m_ref[...]
        m_new = jnp.maximum(m_prev, jnp.max(m_cur, axis=-1, keepdims=True))
        alpha = jnp.exp2(m_prev - m_new)
        p_blocks = [jnp.exp2(sb - m_new) for sb in s_blocks]
        if l_ref is not None:
            l_new = alpha * l_ref[...]
            for pb in p_blocks:
                l_new = l_new + pb
            l_ref[...] = l_new
        p = jnp.concatenate([pb.astype(BF16) for pb in p_blocks], axis=1)
        v = v_ref[pl.ds(start, tk), :]
        alpha_acc = alpha if v_rep == 1 else jnp.concatenate([alpha] * v_rep, axis=1)
        acc_ref[...] = alpha_acc * acc_ref[...] + jnp.dot(p, v, preferred_element_type=F32)
        m_ref[...] = m_new

    scores(0, 0)
    for c in range(n_chunks):
        if c + 1 < n_chunks:
            scores(c + 1, (c + 1) % 2)
        update(c, c % 2)


def _attn_a_kernel(q_ref, k_ref, v_ref, o_ref, qs_ref, s_ref, m_ref, acc_ref,
                   *, group, tk, n_chunks):
    tq = q_ref.shape[0]
    for g in range(group):
        qs_ref[g * tq:(g + 1) * tq, :] = q_ref[:, g * HEAD_DIM:(g + 1) * HEAD_DIM]
    _flash(qs_ref, ((0, group * tq, 0),), k_ref, v_ref, tk, n_chunks, s_ref, m_ref, None, acc_ref)
    for g in range(group):
        rows = slice(g * tq, (g + 1) * tq)
        y = acc_ref[rows, 0:HEAD_DIM] / acc_ref[rows, HEAD_DIM:2 * HEAD_DIM]
        o_ref[:, g * HEAD_DIM:(g + 1) * HEAD_DIM] = y.astype(o_ref.dtype)


def _attn_c_kernel(q_ref, k_ref, v_ref, lam_ref, g_ref, o_ref, qs_ref, s_ref, m_ref, l_ref,
                   acc_ref, *, tk, n_chunks, lambda_init):
    tq = q_ref.shape[0]
    for j in range(2):
        qs_ref[j * tq:(j + 1) * tq, :] = q_ref[:, j * HEAD_DIM:(j + 1) * HEAD_DIM]
    _flash(qs_ref, ((0, tq, 0), (tq, tq, HEAD_DIM)), k_ref, v_ref, tk, n_chunks, s_ref, m_ref,
           l_ref, acc_ref)
    lq = lam_ref[...]
    lam = (jnp.exp(jnp.sum(lq[0:1] * lq[1:2], axis=-1, keepdims=True))
           - jnp.exp(jnp.sum(lq[2:3] * lq[3:4], axis=-1, keepdims=True)) + lambda_init)
    l0 = jnp.sum(l_ref[0:tq, :], axis=-1, keepdims=True)
    l1 = jnp.sum(l_ref[tq:2 * tq, :], axis=-1, keepdims=True)
    y = (acc_ref[0:tq, :] / l0) - lam * (acc_ref[tq:2 * tq, :] / l1)
    ms = jnp.mean(y * y, axis=-1, keepdims=True)
    y = y * lax.rsqrt(ms + EPS) * g_ref[...]
    o_ref[...] = (y * (1.0 - lambda_init)).astype(o_ref.dtype)


def _kv_chunk(kv_len):
    return _tile(kv_len, 1024, 2 * LANES)


def _flash_scratch(rows, tk, v_width, with_sums):
    sums = [pltpu.VMEM((rows, LANES), F32)] if with_sums else []
    return ([pltpu.VMEM((rows, HEAD_DIM), BF16), pltpu.VMEM((2, rows, tk), F32),
             pltpu.VMEM((rows, LANES), F32)] + sums + [pltpu.VMEM((rows, v_width), F32)])


def _attn_a(q, k, v, n_batch, q_len, kv_len, kv_block):
    group = A_Q_HEADS // A_KV_HEADS
    gw = group * HEAD_DIM
    tq = _tile(q_len, 1024 // group, BF16_SUBLANES)
    nq = q_len // tq
    tk = _kv_chunk(kv_len)
    kernel = functools.partial(_attn_a_kernel, group=group, tk=tk, n_chunks=kv_len // tk)
    return pl.pallas_call(
        kernel,
        out_shape=jax.ShapeDtypeStruct(q.shape, BF16),
        grid=(n_batch, A_KV_HEADS, nq),
        in_specs=[pl.BlockSpec((tq, gw), lambda b, h, i: (b * nq + i, h)),
                  pl.BlockSpec((None, kv_len, HEAD_DIM), lambda b, h, i: (b, kv_block, h)),
                  pl.BlockSpec((None, kv_len, 2 * HEAD_DIM), lambda b, h, i: (b, kv_block, h))],
        out_specs=pl.BlockSpec((tq, gw), lambda b, h, i: (b * nq + i, h)),
        scratch_shapes=_flash_scratch(group * tq, tk, 2 * HEAD_DIM, False),
        compiler_params=_params("arbitrary", "arbitrary", "arbitrary"),
        name="attn_gqa",
    )(q, k, v)


def _attn_c(q, k, v, lam_qk, g_subln, lambda_init, n_batch, q_len, kv_len, kv_block):
    hw = 2 * HEAD_DIM
    tq = _tile(q_len, 512, BF16_SUBLANES)
    nq = q_len // tq
    tk = _kv_chunk(kv_len)
    kernel = functools.partial(_attn_c_kernel, tk=tk, n_chunks=kv_len // tk,
                               lambda_init=lambda_init)
    return pl.pallas_call(
        kernel,
        out_shape=jax.ShapeDtypeStruct(q.shape, BF16),
        grid=(n_batch, C_HEADS, nq),
        in_specs=[pl.BlockSpec((tq, hw), lambda b, h, i: (b * nq + i, h)),
                  pl.BlockSpec((None, kv_len, hw), lambda b, h, i: (b, kv_block, h)),
                  pl.BlockSpec((None, kv_len, hw), lambda b, h, i: (b, kv_block, h)),
                  pl.BlockSpec(lam_qk.shape, lambda b, h, i: (0, 0)),
                  pl.BlockSpec((1, hw), lambda b, h, i: (0, 0))],
        out_specs=pl.BlockSpec((tq, hw), lambda b, h, i: (b * nq + i, h)),
        scratch_shapes=_flash_scratch(2 * tq, tk, hw, True),
        compiler_params=_params("arbitrary", "arbitrary", "arbitrary"),
        name="attn_diff",
    )(q, k, v, lam_qk, g_subln.reshape(1, hw))


def _conv3(c, c_prev_row, c_next_row, w_ref):
    tm = c.shape[0]
    row = lax.broadcasted_iota(jnp.int32, (tm, 1), 0)
    prev = jnp.where(row == 0, c_prev_row, pltpu.roll(c, 1, 0))
    nxt = jnp.where(row == tm - 1, c_next_row, pltpu.roll(c, tm - 1, 0))
    return w_ref[0:1, :] * prev + w_ref[1:2, :] * c + w_ref[2:3, :] * nxt


def _halo_rows(prev_val, next_val, tiles_per_seq):
    ti = pl.program_id(0) % tiles_per_seq
    prev_row = jnp.where(ti == 0, 0.0, prev_val)
    next_row = jnp.where(ti == tiles_per_seq - 1, 0.0, next_val)
    return prev_row, next_row


def _shortconv_kernel(bb_ref, bc_ref, bx_ref, bcp_ref, bxp_ref, bcn_ref, bxn_ref, w_ref, o_ref,
                      *, tiles_per_seq):
    last = BF16_SUBLANES - 1
    c = bc_ref[...].astype(F32) * bx_ref[...].astype(F32)
    cp = bcp_ref[last:last + 1, :].astype(F32) * bxp_ref[last:last + 1, :].astype(F32)
    cn = bcn_ref[0:1, :].astype(F32) * bxn_ref[0:1, :].astype(F32)
    cp, cn = _halo_rows(cp, cn, tiles_per_seq)
    y = bb_ref[...].astype(F32) * _conv3(c, cp, cn, w_ref)
    o_ref[...] = y.astype(o_ref.dtype)


def _glu_kernel(u_ref, g_ref, gp_ref, gn_ref, w_ref, o_ref, *, tiles_per_seq):
    last = BF16_SUBLANES - 1
    g = g_ref[...].astype(F32)
    gp, gn = _halo_rows(gp_ref[last:last + 1, :].astype(F32), gn_ref[0:1, :].astype(F32),
                        tiles_per_seq)
    z = _conv3(g, gp, gn, w_ref)
    y = z * jax.nn.sigmoid(z) * u_ref[...].astype(F32)
    o_ref[...] = y.astype(o_ref.dtype)


def _conv_specs(m, tm, wb, col_block):
    hb = tm // BF16_SUBLANES
    n_hb = m // BF16_SUBLANES
    main = pl.BlockSpec((tm, wb), lambda i, j: (i, col_block + j))
    prev = pl.BlockSpec((BF16_SUBLANES, wb),
                        lambda i, j: (jnp.maximum(i * hb - 1, 0), col_block + j))
    nxt = pl.BlockSpec((BF16_SUBLANES, wb),
                       lambda i, j: (jnp.minimum((i + 1) * hb, n_hb - 1), col_block + j))
    return main, prev, nxt


def _shortconv(st, p, w_conv, bb_col, bc_col, bx_col):
    m = p.shape[0]
    width = w_conv.shape[1]
    tm = _tile(st.seq_len, ROW_TILE, BF16_SUBLANES)
    wb = _tile(math.gcd(math.gcd(bb_col, bc_col), math.gcd(bx_col, width)), 1024, LANES)
    bb = pl.BlockSpec((tm, wb), lambda i, j: (i, bb_col // wb + j))
    bc, bcp, bcn = _conv_specs(m, tm, wb, bc_col // wb)
    bx, bxp, bxn = _conv_specs(m, tm, wb, bx_col // wb)
    return pl.pallas_call(
        functools.partial(_shortconv_kernel, tiles_per_seq=st.seq_len // tm),
        out_shape=jax.ShapeDtypeStruct((m, width), BF16),
        grid=(m // tm, width // wb),
        in_specs=[bb, bc, bx, bcp, bxp, bcn, bxn,
                  pl.BlockSpec((w_conv.shape[0], wb), lambda i, j: (0, j))],
        out_specs=pl.BlockSpec((tm, wb), lambda i, j: (i, j)),
        compiler_params=_params("arbitrary", "arbitrary"),
        name="shortconv",
    )(p, p, p, p, p, p, p, w_conv)


def _glu(st, ug, w_conv):
    m = ug.shape[0]
    d_ff = w_conv.shape[1]
    tm = _tile(st.seq_len, ROW_TILE, BF16_SUBLANES)
    wb = _tile(d_ff, 2048, LANES)
    u = pl.BlockSpec((tm, wb), lambda i, j: (i, j))
    g, gp, gn = _conv_specs(m, tm, wb, d_ff // wb)
    return pl.pallas_call(
        functools.partial(_glu_kernel, tiles_per_seq=st.seq_len // tm),
        out_shape=jax.ShapeDtypeStruct((m, d_ff), BF16),
        grid=(m // tm, d_ff // wb),
        in_specs=[u, g, gp, gn, pl.BlockSpec((w_conv.shape[0], wb), lambda i, j: (0, j))],
        out_specs=pl.BlockSpec((tm, wb), lambda i, j: (i, j)),
        compiler_params=_params("arbitrary", "arbitrary"),
        name="glu",
    )(ug, ug, ug, ug, w_conv)


def _branch_kernel(ya_ref, yb_ref, yc_ref, w0_ref, w1_ref, w2_ref, g0_ref, g1_ref, g2_ref, o_ref):
    acc = jax.nn.sigmoid(g0_ref[...].astype(F32)) * jnp.dot(
        ya_ref[...], w0_ref[...], preferred_element_type=F32)
    acc += jax.nn.sigmoid(g1_ref[...].astype(F32)) * jnp.dot(
        yb_ref[...], w1_ref[...], preferred_element_type=F32)
    acc += jax.nn.sigmoid(g2_ref[...].astype(F32)) * jnp.dot(
        yc_ref[...], w2_ref[...], preferred_element_type=F32)
    o_ref[...] = acc.astype(o_ref.dtype)


def _branch_merge(ya, yb, yc, w_branch, layer, p, gates_col):
    m, k = ya.shape
    d = w_branch.shape[3]
    tm = _tile(m, 1024, BF16_SUBLANES)
    tn = _tile(math.gcd(gates_col, d), 512, LANES)
    a_spec = pl.BlockSpec((tm, k), lambda i, j: (i, 0))

    def w_spec(n):
        return pl.BlockSpec((None, None, k, tn), lambda i, j: (layer, n, 0, j))

    def g_spec(n):
        return pl.BlockSpec((tm, tn), lambda i, j: (i, (gates_col + n * d) // tn + j))

    return pl.pallas_call(
        _branch_kernel,
        out_shape=jax.ShapeDtypeStruct((m, d), BF16),
        grid=(m // tm, d // tn),
        in_specs=[a_spec, a_spec, a_spec, w_spec(0), w_spec(1), w_spec(2),
                  g_spec(0), g_spec(1), g_spec(2)],
        out_specs=pl.BlockSpec((tm, tn), lambda i, j: (i, j)),
        compiler_params=_params("arbitrary", "arbitrary"),
        name="branch_merge",
    )(ya, yb, yc, w_branch, w_branch, w_branch, p, p, p)


def _mm_resid_kernel(a_ref, w_ref, x_ref, gate_ref, o_ref):
    y = jnp.dot(a_ref[...], w_ref[...], preferred_element_type=F32)
    o_ref[...] = x_ref[...] + gate_ref[...] * y


def _mm_resid(st, a, w_stack, layer, x, mod, gate_idx):
    m, k = a.shape
    d = w_stack.shape[2]
    tm = _tile(st.rows_per_mod, 1024, BF16_SUBLANES)
    tn = _tile(d, 512 if k <= 4096 else 256, LANES)
    tpm = st.rows_per_mod // tm
    gate_block = gate_idx * (d // tn)
    return pl.pallas_call(
        _mm_resid_kernel,
        out_shape=jax.ShapeDtypeStruct((m, d), F32),
        grid=(m // tm, d // tn),
        in_specs=[pl.BlockSpec((tm, k), lambda i, j: (i, 0)),
                  pl.BlockSpec((None, k, tn), lambda i, j: (layer, 0, j)),
                  pl.BlockSpec((tm, tn), lambda i, j: (i, j)),
                  pl.BlockSpec((None, 1, tn), lambda i, j: (i // tpm, 0, gate_block + j))],
        out_specs=pl.BlockSpec((tm, tn), lambda i, j: (i, j)),
        compiler_params=_params("arbitrary", "arbitrary"),
        name="mm_resid",
    )(a, w_stack, x, mod)


def _rope_tables(n_tok):
    rows = n_tok // GRID_W
    row = jnp.broadcast_to(jnp.arange(rows, dtype=F32)[:, None], (rows, GRID_W)).reshape(-1)
    col = jnp.broadcast_to(jnp.arange(GRID_W, dtype=F32)[None, :], (rows, GRID_W)).reshape(-1)
    quarter = HEAD_DIM // 4
    freqs = ROPE_THETA ** (-jnp.arange(quarter, dtype=F32) / quarter)
    ang_r = row[:, None] * freqs
    ang_c = col[:, None] * freqs
    ang = jnp.concatenate([ang_r, ang_r, ang_c, ang_c], axis=-1)
    lane = jnp.arange(HEAD_DIM)
    sign = jnp.where((lane % (HEAD_DIM // 2)) < quarter, -1.0, 1.0).astype(F32)
    return jnp.cos(ang), jnp.sin(ang) * sign


def kernel(x, c, ctx, c_ctx, w_mod, b_mod, g_norm_mix, g_norm_ffn, w_in, g_qa, g_ka, g_qc, g_kc,
           w_conv_b, lam_qk, g_subln, w_branch, w_out, w_up, w_conv_ffn, w_down):
    n_batch, n_tok, d = x.shape
    ctx_len = ctx.shape[1]
    depth = w_in.shape[0]
    a_q_dim = A_Q_HEADS * HEAD_DIM
    a_kv_dim = A_KV_HEADS * HEAD_DIM
    c_dim = C_HEADS * 2 * HEAD_DIM
    kv_cols = 2 * a_kv_dim + 2 * c_dim
    qk_cols = kv_cols + a_q_dim + c_dim
    bb_col, bc_col, bx_col = qk_cols, qk_cols + B_WIDTH, qk_cols + 2 * B_WIDTH
    gates_col = qk_cols + 3 * B_WIDTH
    kv_tok = n_tok + ctx_len
    assert n_tok % ctx_len == 0, "context keys must form one key block of the combined key array"

    lat = _Stream(n_batch, n_tok, n_tok)
    con = _Stream(n_batch, ctx_len, n_batch * ctx_len)

    mod_rows = 8
    cvec = jnp.zeros((mod_rows, d), F32).at[:n_batch].set(c).at[n_batch].set(c_ctx)
    mod_all = _modvec(cvec, w_mod, b_mod)
    rope = _rope_tables(n_tok)

    q_scale = HEAD_DIM ** -0.5 * math.log2(math.e)
    ak_col, av_col, ck_col, cv_col = 0, a_kv_dim, 2 * a_kv_dim, 2 * a_kv_dim + c_dim
    aq_col, cq_col = kv_cols, kv_cols + a_q_dim
    gk_a, gk_c, gq_a, gq_c = 0, a_kv_dim, a_kv_dim + c_dim, a_kv_dim + c_dim + a_q_dim
    kv_segs = (("norm", ak_col, a_kv_dim, 0, gk_a, 1.0), ("copy_ones", av_col, a_kv_dim, 1, 0, 1.0),
               ("norm", ck_col, c_dim, 2, gk_c, 1.0), ("copy", cv_col, c_dim, 3, 0, 1.0))
    q_segs = (("norm", aq_col, a_q_dim, 4, gq_a, q_scale), ("norm", cq_col, c_dim, 5, gq_c, q_scale))
    kv_widths = (a_kv_dim, 2 * a_kv_dim, c_dim, c_dim)

    w_branch_b = w_branch.astype(BF16)
    w_out_b = w_out.astype(BF16)
    w_down_b = w_down.astype(BF16)

    xs = x.reshape(n_batch * n_tok, d)
    cs = ctx.reshape(n_batch * ctx_len, d)
    for l in range(depth):
        last = l == depth - 1
        lambda_init = 0.8 - 0.6 * math.exp(-0.3 * l)
        mod = mod_all[l]
        mod_x = mod[:n_batch].reshape(n_batch, 1, N_MOD * d)
        mod_c = mod[n_batch:n_batch + 1].reshape(1, 1, N_MOD * d)
        gains = jnp.concatenate([
            jnp.tile(g_ka[l], A_KV_HEADS), jnp.tile(g_kc[l].reshape(-1), C_HEADS),
            jnp.tile(g_qa[l], A_Q_HEADS), jnp.tile(g_qc[l].reshape(-1), C_HEADS)]).reshape(1, -1)

        h_x = _norm_mod(lat, xs, g_norm_mix[l], mod_x, 0, 1)
        h_c = _norm_mod(con, cs, g_norm_mix[l], mod_c, 0, 1)
        p_x = _mm(h_x, w_in, l)
        p_c = _mm(h_c, w_in, l, n_cols=kv_cols if last else None)

        kv_shapes = [(None, ((n_batch, kv_tok, w), w)) for w in kv_widths]
        outs = _prep(lat, p_x, gains, kv_segs + q_segs, qk_cols, rope, kv_shapes, 0,
                     (a_q_dim, c_dim))
        ka, va, kc, vc, qa_x, qc_x = outs
        kv_arrs = [(arr, ((n_batch, kv_tok, w), w)) for arr, w in zip((ka, va, kc, vc), kv_widths)]
        if last:
            ka, va, kc, vc = _prep(con, p_c, gains, kv_segs, kv_cols, None, kv_arrs,
                                   n_tok // _tile(ctx_len, ROW_TILE, BF16_SUBLANES), ())
        else:
            ka, va, kc, vc, qa_c, qc_c = _prep(
                con, p_c, gains, kv_segs + q_segs, qk_cols, None, kv_arrs,
                n_tok // _tile(ctx_len, ROW_TILE, BF16_SUBLANES), (a_q_dim, c_dim))

        def mixer(st, p, qa, qc, x_in, mod_s, q_len, kv_len, kv_block):
            ya = _attn_a(qa, ka, va, n_batch, q_len, kv_len, kv_block)
            yc = _attn_c(qc, kc, vc, lam_qk[l], g_subln[l], lambda_init, n_batch, q_len, kv_len,
                         kv_block)
            yb = _shortconv(st, p, w_conv_b[l], bb_col, bc_col, bx_col)
            merged = _branch_merge(ya, yb, yc, w_branch_b, l, p, gates_col)
            return _mm_resid(st, merged, w_out_b, l, x_in, mod_s, 2)

        def channel_mixer(st, x_in, mod_s):
            f = _norm_mod(st, x_in, g_norm_ffn[l], mod_s, 3, 4)
            ug = _mm(f, w_up, l)
            act = _glu(st, ug, w_conv_ffn[l])
            return _mm_resid(st, act, w_down_b, l, x_in, mod_s, 5)

        xs = mixer(lat, p_x, qa_x, qc_x, xs, mod_x, n_tok, kv_tok, 0)
        if not last:
            cs = mixer(con, p_c, qa_c, qc_c, cs, mod_c, ctx_len, ctx_len, n_tok // ctx_len)
        xs = channel_mixer(lat, xs, mod_x)
        if not last:
            cs = channel_mixer(con, cs, mod_c)
    return xs.reshape(n_batch, n_tok, d)
```

```python
import functools
import math

import jax
import jax.numpy as jnp
from jax import lax
from jax.experimental import pallas as pl
from jax.experimental.pallas import tpu as pltpu

GRID_W = 64
HEAD_DIM = 128
ROPE_THETA = 10000.0
EPS = 1e-6
N_MOD = 6
A_Q_HEADS = 16
A_KV_HEADS = 4
B_WIDTH = 2048
C_HEADS = 8
N_BRANCH = 3

LANES = 128
BF16_SUBLANES = 16
VMEM_LIMIT = 56 << 20
ROW_TILE = 512
F32 = jnp.float32
BF16 = jnp.bfloat16


def _tile(n, target, mult):
    best = None
    t = mult
    while t <= min(n, target):
        if n % t == 0:
            best = t
        t += mult
    return n if best is None else best


def _params(*semantics):
    return pltpu.CompilerParams(dimension_semantics=semantics, vmem_limit_bytes=VMEM_LIMIT)


class _Stream:
    def __init__(self, n_seq, seq_len, rows_per_mod):
        self.n_seq, self.seq_len, self.rows_per_mod = n_seq, seq_len, rows_per_mod
        self.rows = n_seq * seq_len


def _modvec_kernel(c_ref, w_ref, b_ref, o_ref):
    c = c_ref[...]
    s = c * jax.nn.sigmoid(c)
    o_ref[...] = jnp.dot(s.astype(BF16), w_ref[...].astype(BF16),
                         preferred_element_type=F32) + b_ref[...]


def _modvec(cvec, w_mod, b_mod):
    n_layer, d, n = w_mod.shape
    rows = cvec.shape[0]
    tn = _tile(n, 512, LANES)
    return pl.pallas_call(
        _modvec_kernel,
        out_shape=jax.ShapeDtypeStruct((n_layer, rows, n), F32),
        grid=(n_layer, n // tn),
        in_specs=[pl.BlockSpec((rows, d), lambda l, j: (0, 0)),
                  pl.BlockSpec((None, d, tn), lambda l, j: (l, 0, j)),
                  pl.BlockSpec((None, 1, tn), lambda l, j: (l, 0, j))],
        out_specs=pl.BlockSpec((None, rows, tn), lambda l, j: (l, 0, j)),
        compiler_params=_params("arbitrary", "arbitrary"),
        name="modvec",
    )(cvec, w_mod, b_mod.reshape(n_layer, 1, n))


def _norm_mod_kernel(x_ref, g_ref, sh_ref, sc_ref, o_ref):
    x = x_ref[...]
    ms = jnp.mean(x * x, axis=-1, keepdims=True)
    y = x * lax.rsqrt(ms + EPS) * g_ref[...]
    o_ref[...] = (y * (1.0 + sc_ref[...]) + sh_ref[...]).astype(o_ref.dtype)


def _norm_mod(st, x, g, mod, shift_idx, scale_idx):
    m, d = x.shape
    tm = _tile(math.gcd(st.seq_len, st.rows_per_mod), ROW_TILE, BF16_SUBLANES)
    tpm = st.rows_per_mod // tm
    return pl.pallas_call(
        _norm_mod_kernel,
        out_shape=jax.ShapeDtypeStruct((m, d), BF16),
        grid=(m // tm,),
        in_specs=[pl.BlockSpec((tm, d), lambda i: (i, 0)),
                  pl.BlockSpec((1, d), lambda i: (0, 0)),
                  pl.BlockSpec((None, 1, d), lambda i: (i // tpm, 0, shift_idx)),
                  pl.BlockSpec((None, 1, d), lambda i: (i // tpm, 0, scale_idx))],
        out_specs=pl.BlockSpec((tm, d), lambda i: (i, 0)),
        compiler_params=_params("arbitrary"),
        name="norm_mod",
    )(x, g.reshape(1, d), mod, mod)


def _mm_kernel(a_ref, w_ref, o_ref, wb_ref):
    @pl.when(pl.program_id(1) == 0)
    def _():
        wb_ref[...] = w_ref[...].astype(BF16)

    o_ref[...] = jnp.dot(a_ref[...], wb_ref[...], preferred_element_type=F32).astype(o_ref.dtype)


def _mm(a, w_stack, layer, n_cols=None, tm_target=1024, tn_target=512):
    m, k = a.shape
    n = w_stack.shape[2] if n_cols is None else n_cols
    tm = _tile(m, tm_target, BF16_SUBLANES)
    tn = _tile(n, tn_target, LANES)
    return pl.pallas_call(
        _mm_kernel,
        out_shape=jax.ShapeDtypeStruct((m, n), BF16),
        grid=(n // tn, m // tm),
        in_specs=[pl.BlockSpec((tm, k), lambda j, i: (i, 0)),
                  pl.BlockSpec((None, k, tn), lambda j, i: (layer, 0, j))],
        out_specs=pl.BlockSpec((tm, tn), lambda j, i: (i, j)),
        scratch_shapes=[pltpu.VMEM((k, tn), BF16)],
        compiler_params=_params("arbitrary", "arbitrary"),
        name="mm",
    )(a, w_stack)


def _lane_sum_mxu(x):
    hi = x.astype(BF16)
    lo = (x - hi.astype(F32)).astype(BF16)
    ones = jnp.ones((2 * x.shape[1], x.shape[1]), BF16)
    return jnp.dot(jnp.concatenate([hi, lo], axis=1), ones, preferred_element_type=F32)


def _norm_rope_head(x, g, cos, sin_signed, first_half, out_scale):
    ms = _lane_sum_mxu(x * x) * (1.0 / HEAD_DIM)
    y = x * lax.rsqrt(ms + EPS) * g
    if cos is not None:
        rot = jnp.where(first_half, pltpu.roll(y, HEAD_DIM - HEAD_DIM // 4, 1),
                        pltpu.roll(y, HEAD_DIM // 4, 1))
        y = y * cos + rot * sin_signed
    if out_scale != 1.0:
        y = y * out_scale
    return y


def _prep_kernel(*refs, segs, rope):
    n_out = len(segs)
    p_ref, g_ref = refs[0], refs[1]
    if rope:
        cos = refs[2][...]
        sin_signed = refs[3][...]
        outs = refs[4:4 + n_out]
        lane = lax.broadcasted_iota(jnp.int32, (1, HEAD_DIM), 1)
        first_half = (lane % (HEAD_DIM // 2)) < (HEAD_DIM // 4)
    else:
        cos = sin_signed = first_half = None
        outs = refs[2:2 + n_out]
    for (kind, in_col, width, out_index, gain_col, out_scale) in segs:
        o_ref = outs[out_index]
        if kind == "copy":
            o_ref[...] = p_ref[:, in_col:in_col + width]
            continue
        if kind == "copy_ones":
            for h in range(width // HEAD_DIM):
                o_ref[:, 2 * h * HEAD_DIM:(2 * h + 1) * HEAD_DIM] = p_ref[
                    :, in_col + h * HEAD_DIM:in_col + (h + 1) * HEAD_DIM]
                o_ref[:, (2 * h + 1) * HEAD_DIM:(2 * h + 2) * HEAD_DIM] = jnp.ones(
                    (o_ref.shape[0], HEAD_DIM), o_ref.dtype)
            continue
        for h in range(width // HEAD_DIM):
            x = p_ref[:, in_col + h * HEAD_DIM:in_col + (h + 1) * HEAD_DIM].astype(F32)
            g = g_ref[:, gain_col + h * HEAD_DIM:gain_col + (h + 1) * HEAD_DIM]
            y = _norm_rope_head(x, g, cos, sin_signed, first_half, out_scale)
            o_ref[:, h * HEAD_DIM:(h + 1) * HEAD_DIM] = y.astype(o_ref.dtype)


def _prep(st, p, gains, segs, in_width, rope_tables, kv_outs, kv_row_block, q_widths):
    m = p.shape[0]
    tm = _tile(st.seq_len, ROW_TILE, BF16_SUBLANES)
    tps = st.seq_len // tm
    rope = rope_tables is not None
    in_specs = [pl.BlockSpec((tm, in_width), lambda i: (i, 0)),
                pl.BlockSpec((1, gains.shape[1]), lambda i: (0, 0))]
    args = [p, gains]
    if rope:
        in_specs += [pl.BlockSpec((tm, HEAD_DIM), lambda i: (i % tps, 0))] * 2
        args += list(rope_tables)
    out_shapes, out_specs, aliases = [], [], {}
    for arr, (shape, width) in kv_outs:
        out_shapes.append(jax.ShapeDtypeStruct(shape, BF16))
        out_specs.append(pl.BlockSpec((None, tm, width),
                                      lambda i: (i // tps, kv_row_block + i % tps, 0)))
        if arr is not None:
            aliases[len(args)] = len(out_shapes) - 1
            in_specs.append(pl.BlockSpec(memory_space=pl.ANY))
            args.append(arr)
    for width in q_widths:
        out_shapes.append(jax.ShapeDtypeStruct((m, width), BF16))
        out_specs.append(pl.BlockSpec((tm, width), lambda i: (i, 0)))
    return pl.pallas_call(
        functools.partial(_prep_kernel_aliased, segs=segs, rope=rope, n_alias=len(aliases)),
        out_shape=out_shapes,
        grid=(m // tm,),
        in_specs=in_specs,
        out_specs=out_specs,
        input_output_aliases=aliases,
        compiler_params=_params("arbitrary"),
        name="prep",
    )(*args)


def _prep_kernel_aliased(*refs, segs, rope, n_alias):
    n_in = 2 + (2 if rope else 0)
    _prep_kernel(*refs[:n_in], *refs[n_in + n_alias:], segs=segs, rope=rope)


def _flash(q_ref, q_parts, k_ref, v_ref, tk, n_chunks, s_ref, m_ref, l_ref, acc_ref,
           qn_ref=None, is_first=None):
    m_ref[...] = jnp.full(m_ref.shape, -jnp.inf, F32)
    if l_ref is not None:
        l_ref[...] = jnp.zeros(l_ref.shape, F32)
    acc_ref[...] = jnp.zeros(acc_ref.shape, F32)

    def scores(c, slot, src_ref=q_ref):
        start = c * tk
        for row0, n_rows, k_col in q_parts:
            k = k_ref[pl.ds(start, tk), k_col:k_col + HEAD_DIM]
            s_ref[slot, row0:row0 + n_rows, :] = lax.dot_general(
                src_ref[row0:row0 + n_rows, :], k, (((1,), (1,)), ((), ())),
                preferred_element_type=F32)

    n_col = tk // LANES
    v_rep = acc_ref.shape[1] // LANES

    def update(c, slot):
        start = c * tk
        s_blocks = [s_ref[slot, :, j * LANES:(j + 1) * LANES] for j in range(n_col)]
        m_cur = s_blocks[0]
        for sb in s_blocks[1:]:
            m_cur = jnp.maximum(m_cur, sb)
        m_prev = m_ref[...]
        m_new = jnp.maximum(m_prev, jnp.max(m_cur, axis=-1, keepdims=True))
        alpha = jnp.exp2(m_prev - m_new)
        p_blocks = [jnp.exp2(sb - m_new) for sb in s_blocks]
        if l_ref is not None:
            l_new = alpha * l_ref[...]
            for pb in p_blocks:
                l_new = l_new + pb
            l_ref[...] = l_new
        p = jnp.concatenate([pb.astype(BF16) for pb in p_blocks], axis=1)
        v = v_ref[pl.ds(start, tk), :]
        alpha_acc = alpha if v_rep == 1 else jnp.concatenate([alpha] * v_rep, axis=1)
        acc_ref[...] = alpha_acc * acc_ref[...] + jnp.dot(p, v, preferred_element_type=F32)
        m_ref[...] = m_new

    if qn_ref is None or n_chunks < 2:
        scores(0, 0)
        for c in range(n_chunks):
            if c + 1 < n_chunks:
                scores(c + 1, (c + 1) % 2)
            update(c, c % 2)
        return

    def slot_of(c):
        return 2 if c == 0 else (c - 1) % 2

    @pl.when(is_first)
    def _():
        scores(0, 2)

    for c in range(n_chunks):
        if c + 1 < n_chunks:
            scores(c + 1, slot_of(c + 1))
        else:
            scores(0, 2, qn_ref)
        update(c, slot_of(c))


def _attn_a_kernel(q_ref, qn_ref, k_ref, v_ref, o_ref, qs_ref, s_ref, m_ref, acc_ref, qs2_ref,
                   *, group, tk, n_chunks):
    tq = q_ref.shape[0]
    for g in range(group):
        qs_ref[g * tq:(g + 1) * tq, :] = q_ref[:, g * HEAD_DIM:(g + 1) * HEAD_DIM]
        qs2_ref[g * tq:(g + 1) * tq, :] = qn_ref[:, g * HEAD_DIM:(g + 1) * HEAD_DIM]
    _flash(qs_ref, ((0, group * tq, 0),), k_ref, v_ref, tk, n_chunks, s_ref, m_ref, None, acc_ref,
           qn_ref=qs2_ref, is_first=pl.program_id(2) == 0)
    for g in range(group):
        rows = slice(g * tq, (g + 1) * tq)
        y = acc_ref[rows, 0:HEAD_DIM] / acc_ref[rows, HEAD_DIM:2 * HEAD_DIM]
        o_ref[:, g * HEAD_DIM:(g + 1) * HEAD_DIM] = y.astype(o_ref.dtype)


def _attn_c_kernel(q_ref, qn_ref, k_ref, v_ref, lam_ref, g_ref, o_ref, qs_ref, s_ref, m_ref, l_ref,
                   acc_ref, qs2_ref, *, tk, n_chunks, lambda_init):
    tq = q_ref.shape[0]
    for j in range(2):
        qs_ref[j * tq:(j + 1) * tq, :] = q_ref[:, j * HEAD_DIM:(j + 1) * HEAD_DIM]
        qs2_ref[j * tq:(j + 1) * tq, :] = qn_ref[:, j * HEAD_DIM:(j + 1) * HEAD_DIM]
    _flash(qs_ref, ((0, tq, 0), (tq, tq, HEAD_DIM)), k_ref, v_ref, tk, n_chunks, s_ref, m_ref,
           l_ref, acc_ref, qn_ref=qs2_ref, is_first=pl.program_id(2) == 0)
    lq = lam_ref[...]
    lam = (jnp.exp(jnp.sum(lq[0:1] * lq[1:2], axis=-1, keepdims=True))
           - jnp.exp(jnp.sum(lq[2:3] * lq[3:4], axis=-1, keepdims=True)) + lambda_init)
    l0 = jnp.sum(l_ref[0:tq, :], axis=-1, keepdims=True)
    l1 = jnp.sum(l_ref[tq:2 * tq, :], axis=-1, keepdims=True)
    y = (acc_ref[0:tq, :] / l0) - lam * (acc_ref[tq:2 * tq, :] / l1)
    ms = jnp.mean(y * y, axis=-1, keepdims=True)
    y = y * lax.rsqrt(ms + EPS) * g_ref[...]
    o_ref[...] = (y * (1.0 - lambda_init)).astype(o_ref.dtype)


def _kv_chunk(kv_len):
    return _tile(kv_len, 1024, 2 * LANES)


def _flash_scratch(rows, tk, v_width, with_sums, n_slots=2):
    sums = [pltpu.VMEM((rows, LANES), F32)] if with_sums else []
    return ([pltpu.VMEM((rows, HEAD_DIM), BF16), pltpu.VMEM((n_slots, rows, tk), F32),
             pltpu.VMEM((rows, LANES), F32)] + sums + [pltpu.VMEM((rows, v_width), F32)])


def _attn_a(q, k, v, n_batch, q_len, kv_len, kv_block):
    group = A_Q_HEADS // A_KV_HEADS
    gw = group * HEAD_DIM
    tq = _tile(q_len, 1024 // group, BF16_SUBLANES)
    nq = q_len // tq
    tk = _kv_chunk(kv_len)
    kernel = functools.partial(_attn_a_kernel, group=group, tk=tk, n_chunks=kv_len // tk)
    return pl.pallas_call(
        kernel,
        out_shape=jax.ShapeDtypeStruct(q.shape, BF16),
        grid=(n_batch, A_KV_HEADS, nq),
        in_specs=[pl.BlockSpec((tq, gw), lambda b, h, i: (b * nq + i, h)),
                  pl.BlockSpec((tq, gw), lambda b, h, i: (b * nq + jnp.minimum(i + 1, nq - 1), h)),
                  pl.BlockSpec((None, kv_len, HEAD_DIM), lambda b, h, i: (b, kv_block, h)),
                  pl.BlockSpec((None, kv_len, 2 * HEAD_DIM), lambda b, h, i: (b, kv_block, h))],
        out_specs=pl.BlockSpec((tq, gw), lambda b, h, i: (b * nq + i, h)),
        scratch_shapes=(_flash_scratch(group * tq, tk, 2 * HEAD_DIM, False, 3)
                        + [pltpu.VMEM((group * tq, HEAD_DIM), BF16)]),
        compiler_params=_params("arbitrary", "arbitrary", "arbitrary"),
        name="attn_gqa",
    )(q, q, k, v)


def _attn_c(q, k, v, lam_qk, g_subln, lambda_init, n_batch, q_len, kv_len, kv_block):
    hw = 2 * HEAD_DIM
    tq = _tile(q_len, 512, BF16_SUBLANES)
    nq = q_len // tq
    tk = _kv_chunk(kv_len)
    kernel = functools.partial(_attn_c_kernel, tk=tk, n_chunks=kv_len // tk,
                               lambda_init=lambda_init)
    return pl.pallas_call(
        kernel,
        out_shape=jax.ShapeDtypeStruct(q.shape, BF16),
        grid=(n_batch, C_HEADS, nq),
        in_specs=[pl.BlockSpec((tq, hw), lambda b, h, i: (b * nq + i, h)),
                  pl.BlockSpec((tq, hw), lambda b, h, i: (b * nq + jnp.minimum(i + 1, nq - 1), h)),
                  pl.BlockSpec((None, kv_len, hw), lambda b, h, i: (b, kv_block, h)),
                  pl.BlockSpec((None, kv_len, hw), lambda b, h, i: (b, kv_block, h)),
                  pl.BlockSpec(lam_qk.shape, lambda b, h, i: (0, 0)),
                  pl.BlockSpec((1, hw), lambda b, h, i: (0, 0))],
        out_specs=pl.BlockSpec((tq, hw), lambda b, h, i: (b * nq + i, h)),
        scratch_shapes=(_flash_scratch(2 * tq, tk, hw, True, 3)
                        + [pltpu.VMEM((2 * tq, HEAD_DIM), BF16)]),
        compiler_params=_params("arbitrary", "arbitrary", "arbitrary"),
        name="attn_diff",
    )(q, q, k, v, lam_qk, g_subln.reshape(1, hw))


def _conv3(c, c_prev_row, c_next_row, w_ref):
    tm = c.shape[0]
    row = lax.broadcasted_iota(jnp.int32, (tm, 1), 0)
    prev = jnp.where(row == 0, c_prev_row, pltpu.roll(c, 1, 0))
    nxt = jnp.where(row == tm - 1, c_next_row, pltpu.roll(c, tm - 1, 0))
    return w_ref[0:1, :] * prev + w_ref[1:2, :] * c + w_ref[2:3, :] * nxt


def _halo_rows(prev_val, next_val, tiles_per_seq):
    ti = pl.program_id(0) % tiles_per_seq
    prev_row = jnp.where(ti == 0, 0.0, prev_val)
    next_row = jnp.where(ti == tiles_per_seq - 1, 0.0, next_val)
    return prev_row, next_row


def _shortconv_kernel(bb_ref, bc_ref, bx_ref, bcp_ref, bxp_ref, bcn_ref, bxn_ref, w_ref, o_ref,
                      *, tiles_per_seq):
    last = BF16_SUBLANES - 1
    c = bc_ref[...].astype(F32) * bx_ref[...].astype(F32)
    cp = bcp_ref[last:last + 1, :].astype(F32) * bxp_ref[last:last + 1, :].astype(F32)
    cn = bcn_ref[0:1, :].astype(F32) * bxn_ref[0:1, :].astype(F32)
    cp, cn = _halo_rows(cp, cn, tiles_per_seq)
    y = bb_ref[...].astype(F32) * _conv3(c, cp, cn, w_ref)
    o_ref[...] = y.astype(o_ref.dtype)


def _glu_kernel(u_ref, g_ref, gp_ref, gn_ref, w_ref, o_ref, *, tiles_per_seq):
    last = BF16_SUBLANES - 1
    g = g_ref[...].astype(F32)
    gp, gn = _halo_rows(gp_ref[last:last + 1, :].astype(F32), gn_ref[0:1, :].astype(F32),
                        tiles_per_seq)
    z = _conv3(g, gp, gn, w_ref)
    y = z * jax.nn.sigmoid(z) * u_ref[...].astype(F32)
    o_ref[...] = y.astype(o_ref.dtype)


def _conv_specs(m, tm, wb, col_block):
    hb = tm // BF16_SUBLANES
    n_hb = m // BF16_SUBLANES
    main = pl.BlockSpec((tm, wb), lambda i, j: (i, col_block + j))
    prev = pl.BlockSpec((BF16_SUBLANES, wb),
                        lambda i, j: (jnp.maximum(i * hb - 1, 0), col_block + j))
    nxt = pl.BlockSpec((BF16_SUBLANES, wb),
                       lambda i, j: (jnp.minimum((i + 1) * hb, n_hb - 1), col_block + j))
    return main, prev, nxt


def _shortconv(st, p, w_conv, bb_col, bc_col, bx_col):
    m = p.shape[0]
    width = w_conv.shape[1]
    tm = _tile(st.seq_len, ROW_TILE, BF16_SUBLANES)
    wb = _tile(math.gcd(math.gcd(bb_col, bc_col), math.gcd(bx_col, width)), 1024, LANES)
    bb = pl.BlockSpec((tm, wb), lambda i, j: (i, bb_col // wb + j))
    bc, bcp, bcn = _conv_specs(m, tm, wb, bc_col // wb)
    bx, bxp, bxn = _conv_specs(m, tm, wb, bx_col // wb)
    return pl.pallas_call(
        functools.partial(_shortconv_kernel, tiles_per_seq=st.seq_len // tm),
        out_shape=jax.ShapeDtypeStruct((m, width), BF16),
        grid=(m // tm, width // wb),
        in_specs=[bb, bc, bx, bcp, bxp, bcn, bxn,
                  pl.BlockSpec((w_conv.shape[0], wb), lambda i, j: (0, j))],
        out_specs=pl.BlockSpec((tm, wb), lambda i, j: (i, j)),
        compiler_params=_params("arbitrary", "arbitrary"),
        name="shortconv",
    )(p, p, p, p, p, p, p, w_conv)


def _glu(st, ug, w_conv):
    m = ug.shape[0]
    d_ff = w_conv.shape[1]
    tm = _tile(st.seq_len, ROW_TILE, BF16_SUBLANES)
    wb = _tile(d_ff, 2048, LANES)
    u = pl.BlockSpec((tm, wb), lambda i, j: (i, j))
    g, gp, gn = _conv_specs(m, tm, wb, d_ff // wb)
    return pl.pallas_call(
        functools.partial(_glu_kernel, tiles_per_seq=st.seq_len // tm),
        out_shape=jax.ShapeDtypeStruct((m, d_ff), BF16),
        grid=(m // tm, d_ff // wb),
        in_specs=[u, g, gp, gn, pl.BlockSpec((w_conv.shape[0], wb), lambda i, j: (0, j))],
        out_specs=pl.BlockSpec((tm, wb), lambda i, j: (i, j)),
        compiler_params=_params("arbitrary", "arbitrary"),
        name="glu",
    )(ug, ug, ug, ug, w_conv)


def _branch_kernel(ya_ref, yb_ref, yc_ref, w0_ref, w1_ref, w2_ref, g0_ref, g1_ref, g2_ref, o_ref):
    acc = jax.nn.sigmoid(g0_ref[...].astype(F32)) * jnp.dot(
        ya_ref[...], w0_ref[...], preferred_element_type=F32)
    acc += jax.nn.sigmoid(g1_ref[...].astype(F32)) * jnp.dot(
        yb_ref[...], w1_ref[...], preferred_element_type=F32)
    acc += jax.nn.sigmoid(g2_ref[...].astype(F32)) * jnp.dot(
        yc_ref[...], w2_ref[...], preferred_element_type=F32)
    o_ref[...] = acc.astype(o_ref.dtype)


def _branch_merge(ya, yb, yc, w_branch, layer, p, gates_col):
    m, k = ya.shape
    d = w_branch.shape[3]
    tm = _tile(m, 1024, BF16_SUBLANES)
    tn = _tile(math.gcd(gates_col, d), 512, LANES)
    a_spec = pl.BlockSpec((tm, k), lambda i, j: (i, 0))

    def w_spec(n):
        return pl.BlockSpec((None, None, k, tn), lambda i, j: (layer, n, 0, j))

    def g_spec(n):
        return pl.BlockSpec((tm, tn), lambda i, j: (i, (gates_col + n * d) // tn + j))

    return pl.pallas_call(
        _branch_kernel,
        out_shape=jax.ShapeDtypeStruct((m, d), BF16),
        grid=(m // tm, d // tn),
        in_specs=[a_spec, a_spec, a_spec, w_spec(0), w_spec(1), w_spec(2),
                  g_spec(0), g_spec(1), g_spec(2)],
        out_specs=pl.BlockSpec((tm, tn), lambda i, j: (i, j)),
        compiler_params=_params("arbitrary", "arbitrary"),
        name="branch_merge",
    )(ya, yb, yc, w_branch, w_branch, w_branch, p, p, p)


def _mm_resid_kernel(a_ref, w_ref, x_ref, gate_ref, o_ref):
    y = jnp.dot(a_ref[...], w_ref[...], preferred_element_type=F32)
    o_ref[...] = x_ref[...] + gate_ref[...] * y


def _mm_resid(st, a, w_stack, layer, x, mod, gate_idx):
    m, k = a.shape
    d = w_stack.shape[2]
    tm = _tile(st.rows_per_mod, 1024, BF16_SUBLANES)
    tn = _tile(d, 512 if k <= 4096 else 256, LANES)
    tpm = st.rows_per_mod // tm
    gate_block = gate_idx * (d // tn)
    return pl.pallas_call(
        _mm_resid_kernel,
        out_shape=jax.ShapeDtypeStruct((m, d), F32),
        grid=(m // tm, d // tn),
        in_specs=[pl.BlockSpec((tm, k), lambda i, j: (i, 0)),
                  pl.BlockSpec((None, k, tn), lambda i, j: (layer, 0, j)),
                  pl.BlockSpec((tm, tn), lambda i, j: (i, j)),
                  pl.BlockSpec((None, 1, tn), lambda i, j: (i // tpm, 0, gate_block + j))],
        out_specs=pl.BlockSpec((tm, tn), lambda i, j: (i, j)),
        compiler_params=_params("arbitrary", "arbitrary"),
        name="mm_resid",
    )(a, w_stack, x, mod)


def _rope_tables(n_tok):
    rows = n_tok // GRID_W
    row = jnp.broadcast_to(jnp.arange(rows, dtype=F32)[:, None], (rows, GRID_W)).reshape(-1)
    col = jnp.broadcast_to(jnp.arange(GRID_W, dtype=F32)[None, :], (rows, GRID_W)).reshape(-1)
    quarter = HEAD_DIM // 4
    freqs = ROPE_THETA ** (-jnp.arange(quarter, dtype=F32) / quarter)
    ang_r = row[:, None] * freqs
    ang_c = col[:, None] * freqs
    ang = jnp.concatenate([ang_r, ang_r, ang_c, ang_c], axis=-1)
    lane = jnp.arange(HEAD_DIM)
    sign = jnp.where((lane % (HEAD_DIM // 2)) < quarter, -1.0, 1.0).astype(F32)
    return jnp.cos(ang), jnp.sin(ang) * sign


def kernel(x, c, ctx, c_ctx, w_mod, b_mod, g_norm_mix, g_norm_ffn, w_in, g_qa, g_ka, g_qc, g_kc,
           w_conv_b, lam_qk, g_subln, w_branch, w_out, w_up, w_conv_ffn, w_down):
    n_batch, n_tok, d = x.shape
    ctx_len = ctx.shape[1]
    depth = w_in.shape[0]
    a_q_dim = A_Q_HEADS * HEAD_DIM
    a_kv_dim = A_KV_HEADS * HEAD_DIM
    c_dim = C_HEADS * 2 * HEAD_DIM
    kv_cols = 2 * a_kv_dim + 2 * c_dim
    qk_cols = kv_cols + a_q_dim + c_dim
    bb_col, bc_col, bx_col = qk_cols, qk_cols + B_WIDTH, qk_cols + 2 * B_WIDTH
    gates_col = qk_cols + 3 * B_WIDTH
    kv_tok = n_tok + ctx_len
    assert n_tok % ctx_len == 0, "context keys must form one key block of the combined key array"

    lat = _Stream(n_batch, n_tok, n_tok)
    con = _Stream(n_batch, ctx_len, n_batch * ctx_len)

    mod_rows = 8
    cvec = jnp.zeros((mod_rows, d), F32).at[:n_batch].set(c).at[n_batch].set(c_ctx)
    mod_all = _modvec(cvec, w_mod, b_mod)
    rope = _rope_tables(n_tok)

    q_scale = HEAD_DIM ** -0.5 * math.log2(math.e)
    ak_col, av_col, ck_col, cv_col = 0, a_kv_dim, 2 * a_kv_dim, 2 * a_kv_dim + c_dim
    aq_col, cq_col = kv_cols, kv_cols + a_q_dim
    gk_a, gk_c, gq_a, gq_c = 0, a_kv_dim, a_kv_dim + c_dim, a_kv_dim + c_dim + a_q_dim
    kv_segs = (("norm", ak_col, a_kv_dim, 0, gk_a, 1.0), ("copy_ones", av_col, a_kv_dim, 1, 0, 1.0),
               ("norm", ck_col, c_dim, 2, gk_c, 1.0), ("copy", cv_col, c_dim, 3, 0, 1.0))
    q_segs = (("norm", aq_col, a_q_dim, 4, gq_a, q_scale), ("norm", cq_col, c_dim, 5, gq_c, q_scale))
    kv_widths = (a_kv_dim, 2 * a_kv_dim, c_dim, c_dim)

    w_branch_b = w_branch.astype(BF16)
    w_out_b = w_out.astype(BF16)
    w_down_b = w_down.astype(BF16)

    xs = x.reshape(n_batch * n_tok, d)
    cs = ctx.reshape(n_batch * ctx_len, d)
    for l in range(depth):
        last = l == depth - 1
        lambda_init = 0.8 - 0.6 * math.exp(-0.3 * l)
        mod = mod_all[l]
        mod_x = mod[:n_batch].reshape(n_batch, 1, N_MOD * d)
        mod_c = mod[n_batch:n_batch + 1].reshape(1, 1, N_MOD * d)
        gains = jnp.concatenate([
            jnp.tile(g_ka[l], A_KV_HEADS), jnp.tile(g_kc[l].reshape(-1), C_HEADS),
            jnp.tile(g_qa[l], A_Q_HEADS), jnp.tile(g_qc[l].reshape(-1), C_HEADS)]).reshape(1, -1)

        h_x = _norm_mod(lat, xs, g_norm_mix[l], mod_x, 0, 1)
        h_c = _norm_mod(con, cs, g_norm_mix[l], mod_c, 0, 1)
        p_x = _mm(h_x, w_in, l)
        p_c = _mm(h_c, w_in, l, n_cols=kv_cols if last else None)

        kv_shapes = [(None, ((n_batch, kv_tok, w), w)) for w in kv_widths]
        outs = _prep(lat, p_x, gains, kv_segs + q_segs, qk_cols, rope, kv_shapes, 0,
                     (a_q_dim, c_dim))
        ka, va, kc, vc, qa_x, qc_x = outs
        kv_arrs = [(arr, ((n_batch, kv_tok, w), w)) for arr, w in zip((ka, va, kc, vc), kv_widths)]
        if last:
            ka, va, kc, vc = _prep(con, p_c, gains, kv_segs, kv_cols, None, kv_arrs,
                                   n_tok // _tile(ctx_len, ROW_TILE, BF16_SUBLANES), ())
        else:
            ka, va, kc, vc, qa_c, qc_c = _prep(
                con, p_c, gains, kv_segs + q_segs, qk_cols, None, kv_arrs,
                n_tok // _tile(ctx_len, ROW_TILE, BF16_SUBLANES), (a_q_dim, c_dim))

        def mixer(st, p, qa, qc, x_in, mod_s, q_len, kv_len, kv_block):
            ya = _attn_a(qa, ka, va, n_batch, q_len, kv_len, kv_block)
            yc = _attn_c(qc, kc, vc, lam_qk[l], g_subln[l], lambda_init, n_batch, q_len, kv_len,
                         kv_block)
            yb = _shortconv(st, p, w_conv_b[l], bb_col, bc_col, bx_col)
            merged = _branch_merge(ya, yb, yc, w_branch_b, l, p, gates_col)
            return _mm_resid(st, merged, w_out_b, l, x_in, mod_s, 2)

        def channel_mixer(st, x_in, mod_s):
            f = _norm_mod(st, x_in, g_norm_ffn[l], mod_s, 3, 4)
            ug = _mm(f, w_up, l)
            act = _glu(st, ug, w_conv_ffn[l])
            return _mm_resid(st, act, w_down_b, l, x_in, mod_s, 5)

        xs = mixer(lat, p_x, qa_x, qc_x, xs, mod_x, n_tok, kv_tok, 0)
        if not last:
            cs = mixer(con, p_c, qa_c, qc_c, cs, mod_c, ctx_len, ctx_len, n_tok // ctx_len)
        xs = channel_mixer(lat, xs, mod_x)
        if not last:
            cs = channel_mixer(con, cs, mod_c)
    return xs.reshape(n_batch, n_tok, d)
```
